```python
import jax, jax.numpy as jnp
from jax import lax
import numpy as np

D_MODEL = 1024
BATCH = 2
SEQ = 16384
DEPTH = 2
DEC_BATCH = 16
DEC_SEQ = 16
PAST_LEN = 1024

CHUNK = 64
GDN_HEADS = 4
GDN_HEAD_DIM = 128
GDN_WIDTH = GDN_HEADS * GDN_HEAD_DIM
SC_WIDTH = D_MODEL - GDN_WIDTH
GDN_CONV = 4
SC_CONV = 3
D_FF = -(-8 * D_MODEL // (3 * 256)) * 256
EPS = 1e-6
OFF_QKV = 3 * GDN_WIDTH
OFF_Z = OFF_QKV + GDN_WIDTH
OFF_B = OFF_Z + GDN_HEADS
OFF_A = OFF_B + GDN_HEADS
OFF_SB = OFF_A + SC_WIDTH
OFF_SC = OFF_SB + SC_WIDTH
IN_DIM = OFF_SC + SC_WIDTH

kernel_name = 'hybrid_gdn_shortconv_stream_step'


def rmsnorm(x, g):
    xf = x.astype(jnp.float32)
    y = xf * lax.rsqrt(jnp.mean(xf * xf, axis=-1, keepdims=True) + EPS)
    return (y * g.astype(jnp.float32)).astype(x.dtype)


def l2norm(x):
    return x * lax.rsqrt(jnp.sum(x * x, axis=-1, keepdims=True) + EPS)


def causal_dwconv(x, prev, w):
    width = w.shape[0]
    L = x.shape[1]
    xp = jnp.concatenate([prev.astype(x.dtype), x], axis=1)
    y = xp[:, 0:L] * w[0]
    for i in range(1, width):
        y = y + xp[:, i:i + L] * w[i]
    return y, xp[:, xp.shape[1] - (width - 1):]


def gated_delta_rule(q, k, v, g, beta, S0, chunk):
    B, L, H, DK = q.shape
    DV = v.shape[-1]
    N = L // chunk

    def blk(t):
        return jnp.moveaxis(t.reshape((B, N, chunk, H) + t.shape[3:]), 3, 2)

    q, k, v, beta = blk(q), blk(k), blk(v), blk(beta)
    g = jnp.cumsum(blk(g), axis=-1)
    idx = jnp.arange(chunk)
    causal = idx[:, None] >= idx[None, :]
    strict = idx[:, None] > idx[None, :]
    decay = jnp.exp(jnp.where(causal, g[..., :, None] - g[..., None, :], -jnp.inf))
    kb = k * beta[..., None]
    A = jnp.where(strict, jnp.einsum('bnhid,bnhjd->bnhij', kb, k) * decay, 0.0)
    eye = jnp.eye(chunk, dtype=jnp.float32)
    rhs = jnp.concatenate([v * beta[..., None], kb * jnp.exp(g)[..., None]], axis=-1)
    sol = lax.linalg.triangular_solve(eye + A, rhs, left_side=True, lower=True, unit_diagonal=True)
    u, w = sol[..., :DV], sol[..., DV:]
    qk = jnp.einsum('bnhid,bnhjd->bnhij', q, k) * decay
    qg = q * jnp.exp(g)[..., None]
    g_last = g[..., -1]
    k_tail = k * jnp.exp(g_last[..., None] - g)[..., None]
    a_last = jnp.exp(g_last)

    def step(S, xs):
        qg_c, qk_c, u_c, w_c, kt_c, al_c = xs
        v_new = u_c - jnp.einsum('bhcd,bhde->bhce', w_c, S)
        o = jnp.einsum('bhcd,bhde->bhce', qg_c, S) + jnp.einsum('bhij,bhje->bhie', qk_c, v_new)
        S = S * al_c[..., None, None] + jnp.einsum('bhcd,bhce->bhde', kt_c, v_new)
        return S, o

    xs = tuple(jnp.moveaxis(t, 1, 0) for t in (qg, qk, u, w, k_tail, a_last))
    S, o = lax.scan(step, S0.astype(jnp.float32), xs)
    o = o.transpose(1, 0, 3, 2, 4).reshape(B, L, H, DV)
    return o, S


def hybrid_layer(x, conv_prev, S0, sc_prev, norm_mix_pre, w_in, conv_qkv_w, a_log, dt_bias,
                 gdn_norm_w, conv_sc_w, w_o, norm_mix_post, norm_ffn_pre, w_gate, w_up, w_down,
                 norm_ffn_post):
    B, L, _ = x.shape
    chunk = min(CHUNK, L)
    h = rmsnorm(x, norm_mix_pre)
    P = h @ w_in
    qkv_in = P[..., :OFF_QKV]
    z = P[..., OFF_QKV:OFF_Z]
    b_raw = P[..., OFF_Z:OFF_B].astype(jnp.float32)
    a_raw = P[..., OFF_B:OFF_A].astype(jnp.float32)
    sc_b = P[..., OFF_A:OFF_SB]
    sc_c = P[..., OFF_SB:OFF_SC]
    sc_h = P[..., OFF_SC:]

    qkv, conv_new = causal_dwconv(qkv_in, conv_prev, conv_qkv_w)
    qkv = jax.nn.silu(qkv.astype(jnp.float32))
    q = l2norm(qkv[..., :GDN_WIDTH].reshape(B, L, GDN_HEADS, GDN_HEAD_DIM)) * (GDN_HEAD_DIM ** -0.5)
    k = l2norm(qkv[..., GDN_WIDTH:2 * GDN_WIDTH].reshape(B, L, GDN_HEADS, GDN_HEAD_DIM))
    v = qkv[..., 2 * GDN_WIDTH:].reshape(B, L, GDN_HEADS, GDN_HEAD_DIM)
    beta = jax.nn.sigmoid(b_raw)
    g = -jnp.exp(a_log.astype(jnp.float32)) * jax.nn.softplus(a_raw + dt_bias.astype(jnp.float32))
    o, S_new = gated_delta_rule(q, k, v, g, beta, S0, chunk)
    zf = z.astype(jnp.float32).reshape(B, L, GDN_HEADS, GDN_HEAD_DIM)
    o = (rmsnorm(o, gdn_norm_w) * jax.nn.silu(zf)).reshape(B, L, GDN_WIDTH).astype(x.dtype)

    sc_y, sc_new = causal_dwconv(sc_c * sc_h, sc_prev, conv_sc_w)
    sc_out = sc_b * sc_y

    mix = jnp.concatenate([o, sc_out.astype(x.dtype)], axis=-1) @ w_o
    x = x + rmsnorm(mix, norm_mix_post)

    h = rmsnorm(x, norm_ffn_pre)
    f = (jax.nn.silu(h @ w_gate) * (h @ w_up)) @ w_down
    x = x + rmsnorm(f, norm_ffn_post)
    return x, conv_new, S_new, sc_new


def setup_inputs(seed: int = 0) -> dict:
    key = jax.random.key(seed)
    ks = jax.random.split(key, 24)
    f32 = jnp.float32

    def nrm(k, shape, scale):
        return jax.random.normal(k, shape, f32) * scale

    def gain(k, shape):
        return 1.0 + 0.02 * jax.random.normal(k, shape, f32)

    dt = jnp.exp(jax.random.uniform(ks[9], (DEPTH, GDN_HEADS), f32, np.log(1e-3), np.log(1e-1)))
    return {
        'x_prompt': nrm(ks[0], (BATCH, SEQ, D_MODEL), 1.0),
        'x_sample': nrm(ks[1], (DEC_BATCH, DEC_SEQ, D_MODEL), 1.0),
        'cache_gdn_conv': nrm(ks[2], (DEPTH, DEC_BATCH, GDN_CONV - 1, 3 * GDN_WIDTH), 1.0),
        'state_gdn': nrm(ks[3], (DEPTH, DEC_BATCH, GDN_HEADS, GDN_HEAD_DIM, GDN_HEAD_DIM), 0.05),
        'cache_sc_conv': nrm(ks[4], (DEPTH, DEC_BATCH, SC_CONV - 1, SC_WIDTH), 1.0),
        'norm_mix_pre': gain(ks[5], (DEPTH, D_MODEL)),
        'w_in': nrm(ks[6], (DEPTH, D_MODEL, IN_DIM), D_MODEL ** -0.5),
        'conv_qkv_w': nrm(ks[7], (DEPTH, GDN_CONV, 3 * GDN_WIDTH), GDN_CONV ** -0.5),
        'a_log': jnp.log(jax.random.uniform(ks[8], (DEPTH, GDN_HEADS), f32, 1.0, 16.0)),
        'dt_bias': dt + jnp.log(-jnp.expm1(-dt)),
        'gdn_norm_w': gain(ks[10], (DEPTH, GDN_HEAD_DIM)),
        'conv_sc_w': nrm(ks[11], (DEPTH, SC_CONV, SC_WIDTH), SC_CONV ** -0.5),
        'w_o': nrm(ks[12], (DEPTH, D_MODEL, D_MODEL), D_MODEL ** -0.5),
        'norm_mix_post': gain(ks[13], (DEPTH, D_MODEL)),
        'norm_ffn_pre': gain(ks[14], (DEPTH, D_MODEL)),
        'w_gate': nrm(ks[15], (DEPTH, D_MODEL, D_FF), D_MODEL ** -0.5),
        'w_up': nrm(ks[16], (DEPTH, D_MODEL, D_FF), D_MODEL ** -0.5),
        'w_down': nrm(ks[17], (DEPTH, D_FF, D_MODEL), D_FF ** -0.5),
        'norm_ffn_post': gain(ks[18], (DEPTH, D_MODEL)),
    }


def reference(x_prompt, x_sample, cache_gdn_conv, state_gdn, cache_sc_conv, norm_mix_pre, w_in,
              conv_qkv_w, a_log, dt_bias, gdn_norm_w, conv_sc_w, w_o, norm_mix_post, norm_ffn_pre,
              w_gate, w_up, w_down, norm_ffn_post):
    params = (norm_mix_pre, w_in, conv_qkv_w, a_log, dt_bias, gdn_norm_w, conv_sc_w, w_o,
              norm_mix_post, norm_ffn_pre, w_gate, w_up, w_down, norm_ffn_post)

    def run(x, conv0, s0, sc0):
        convs, states, scs = [], [], []
        for l in range(DEPTH):
            x, c, s, sc = hybrid_layer(x, conv0[l], s0[l], sc0[l], *[p[l] for p in params])
            convs.append(c)
            states.append(s)
            scs.append(sc)
        return x, jnp.stack(convs), jnp.stack(states), jnp.stack(scs)

    Bp = x_prompt.shape[0]
    zc = jnp.zeros((DEPTH, Bp, GDN_CONV - 1, 3 * GDN_WIDTH), x_prompt.dtype)
    zs = jnp.zeros((DEPTH, Bp, GDN_HEADS, GDN_HEAD_DIM, GDN_HEAD_DIM), jnp.float32)
    zsc = jnp.zeros((DEPTH, Bp, SC_CONV - 1, SC_WIDTH), x_prompt.dtype)
    y_prompt, conv_p, state_p, sc_p = run(x_prompt, zc, zs, zsc)
    y_sample, conv_s, state_s, sc_s = run(x_sample, cache_gdn_conv, state_gdn, cache_sc_conv)
    return (y_prompt, y_sample, conv_p, state_p, sc_p, conv_s, state_s, sc_s)
```

```python
import functools

import jax
import jax.numpy as jnp
from jax import lax
from jax.experimental import pallas as pl
from jax.experimental.pallas import tpu as pltpu

F32 = jnp.float32
BF16 = jnp.bfloat16

D_MODEL = 1024
HEADS = 4
HEAD_DIM = 128
GDN_W = HEADS * HEAD_DIM
SC_W = D_MODEL - GDN_W
QKV_W = 3 * GDN_W
GDN_CONV = 4
SC_CONV = 3
D_FF = 2816
CHUNK = 64
EPS = 1e-6

COL_Z = QKV_W
COL_B = COL_Z + GDN_W
COL_C = COL_B + SC_W
COL_H = COL_C + SC_W
COL_BA = COL_H + SC_W
LANES = 128
SUBLANES = 8
W_IN_COLS = COL_BA + LANES
A_LANE0 = HEADS

VMEM_LIMIT_BYTES = 56 * 1024 * 1024
FFN_COLS = 704


def _rms(x, g):
    return x * lax.rsqrt(jnp.mean(x * x, axis=-1, keepdims=True) + EPS) * g


def _silu(x):
    return x * jax.nn.sigmoid(x)


def _mm(a, b):
    return jnp.dot(a, b, preferred_element_type=F32)


def _mm_nt(a, b):
    return lax.dot_general(a, b, (((1,), (1,)), ((), ())), preferred_element_type=F32)


def _mm_tn(a, b):
    return lax.dot_general(a, b, (((0,), (0,)), ((), ())), preferred_element_type=F32)


def _split2(x):
    hi = x.astype(BF16)
    lo = (x - hi.astype(F32)).astype(BF16)
    return hi, lo


def _split3(x):
    h1 = x.astype(BF16)
    r1 = x - h1.astype(F32)
    h2 = r1.astype(BF16)
    h3 = (r1 - h2.astype(F32)).astype(BF16)
    return h1, h2, h3


def _mm3(a, b):
    ah, al = _split2(a)
    bh, bl = _split2(b)
    return _mm(ah, bh) + (_mm(ah, bl) + _mm(al, bh))


def _unit_lower_inverse(a_strict, row, col, size):
    eye = (row == col).astype(F32)
    t = eye - jnp.where((row >> 1) == (col >> 1), a_strict, 0.0)
    shift = 1
    while (1 << shift) < size:
        off = ((row >> (shift + 1)) == (col >> (shift + 1))) & ((row >> shift) != (col >> shift))
        t = t - _mm3(t, _mm3(jnp.where(off, a_strict, 0.0), t))
        shift += 1
    return t


def _mixer_kernel(x_ref, cprev_ref, s0_ref, scprev_ref, gpre_ref, win_ref, cw_ref, alog_ref,
                  dtb_ref, gnw_ref, scw_ref, wo_ref, gpost_ref,
                  y_ref, cnew_ref, snew_ref, scnew_ref,
                  s_scr, cbuf, scbuf, q_scr, k_scr, v_scr, g_scr, beta_scr, o_scr,
                  *, nb, tile, chunk):
    rows = nb * tile
    head0 = SUBLANES

    @pl.when(pl.program_id(1) == 0)
    def _():
        s_scr[...] = s0_ref[...]
        for b in range(nb):
            cbuf[b, head0 - (GDN_CONV - 1):head0, :] = cprev_ref[b]
            scbuf[b, head0 - (SC_CONV - 1):head0, :] = scprev_ref[b]

    x = x_ref[...].reshape(rows, D_MODEL)
    h = _rms(x, gpre_ref[...]).astype(BF16)

    def proj(lo, hi):
        return _mm(h, win_ref[:, lo:hi])

    qkv_in = proj(0, QKV_W)
    cw = cw_ref[...]
    for b in range(nb):
        r0 = b * tile
        cbuf[b, head0:head0 + tile, :] = qkv_in[r0:r0 + tile]
        first = head0 - (GDN_CONV - 1)
        acc = cbuf[b, first:first + tile, :] * cw[0:1]
        for i in range(1, GDN_CONV):
            acc = acc + cbuf[b, first + i:first + i + tile, :] * cw[i:i + 1]
        tail = cbuf[b, first + tile:head0 + tile, :]
        cnew_ref[b] = tail
        cbuf[b, first:head0, :] = tail
        act = _silu(acc)
        for hh in range(HEADS):
            lo = hh * HEAD_DIM
            qh = act[:, lo:lo + HEAD_DIM]
            kh = act[:, GDN_W + lo:GDN_W + lo + HEAD_DIM]
            q_scr[r0:r0 + tile, lo:lo + HEAD_DIM] = (
                qh * lax.rsqrt(jnp.sum(qh * qh, axis=-1, keepdims=True) + EPS) * (HEAD_DIM ** -0.5))
            k_scr[r0:r0 + tile, lo:lo + HEAD_DIM] = (
                kh * lax.rsqrt(jnp.sum(kh * kh, axis=-1, keepdims=True) + EPS))
        v_scr[r0:r0 + tile, :] = act[:, 2 * GDN_W:]

    pba = proj(COL_BA, W_IN_COLS)
    beta_scr[...] = jax.nn.sigmoid(pba)
    ga = pba + dtb_ref[...]
    softplus = jnp.maximum(ga, 0.0) + jnp.log1p(jnp.exp(-jnp.abs(ga)))
    g_scr[...] = -jnp.exp(alog_ref[...]) * softplus

    row = lax.broadcasted_iota(jnp.int32, (chunk, chunk), 0)
    col = lax.broadcasted_iota(jnp.int32, (chunk, chunk), 1)
    causal = row >= col
    strict = row > col
    tri = causal.astype(BF16)
    sel_r = lax.broadcasted_iota(jnp.int32, (SUBLANES, LANES), 0)
    sel_c = lax.broadcasted_iota(jnp.int32, (SUBLANES, LANES), 1)
    sel = (sel_c == sel_r + A_LANE0).astype(BF16)

    def chunk_step(c, carry):
        for b in range(nb):
            r0 = pl.multiple_of(b * tile + c * chunk, chunk)
            g3 = _split3(g_scr[pl.ds(r0, chunk), :])
            gc = _mm(tri, g3[0]) + (_mm(tri, g3[1]) + _mm(tri, g3[2]))
            gc3 = _split3(gc)
            gct = _mm_nt(sel, gc3[0]) + (_mm_nt(sel, gc3[1]) + _mm_nt(sel, gc3[2]))
            beta = beta_scr[pl.ds(r0, chunk), :]
            for hh in range(HEADS):
                lo = hh * HEAD_DIM
                gcol = gc[:, A_LANE0 + hh:A_LANE0 + hh + 1]
                grow = gct[hh:hh + 1, :]
                decay = jnp.exp(jnp.where(causal, gcol - grow, -jnp.inf))
                bcol = beta[:, hh:hh + 1]
                qh = q_scr[pl.ds(r0, chunk), lo:lo + HEAD_DIM]
                kh = k_scr[pl.ds(r0, chunk), lo:lo + HEAD_DIM]
                vh = v_scr[pl.ds(r0, chunk), lo:lo + HEAD_DIM]
                kb = kh * bcol
                k16 = kh.astype(BF16)
                a_mat = jnp.where(strict, _mm_nt(kb.astype(BF16), k16) * decay, 0.0)
                qk = _mm_nt(qh.astype(BF16), k16) * decay
                eg = jnp.exp(gcol)
                rhs = jnp.concatenate([vh * bcol, kb * eg], axis=1)
                sol = _mm3(_unit_lower_inverse(a_mat, row, col, chunk), rhs)
                u = sol[:, :HEAD_DIM]
                w = sol[:, HEAD_DIM:]
                glast = gcol[chunk - 1:chunk, :]
                kt = kh * jnp.exp(glast - gcol)
                s_old = s_scr[b, hh]
                ws = _mm(jnp.concatenate([w, qh * eg], axis=0).astype(BF16), s_old.astype(BF16))
                v_new = (u - ws[:chunk]).astype(BF16)
                o_scr[pl.ds(r0, chunk), lo:lo + HEAD_DIM] = ws[chunk:] + _mm(qk.astype(BF16), v_new)
                s_scr[b, hh] = s_old * jnp.exp(glast) + _mm_tn(kt.astype(BF16), v_new)
        return carry

    lax.fori_loop(0, tile // chunk, chunk_step, 0)
    snew_ref[...] = s_scr[...]

    z = proj(COL_Z, COL_B)
    gnw = gnw_ref[...]
    og = []
    for hh in range(HEADS):
        lo = hh * HEAD_DIM
        og.append(_rms(o_scr[:, lo:lo + HEAD_DIM], gnw) * _silu(z[:, lo:lo + HEAD_DIM]))
    og = jnp.concatenate(og, axis=1).astype(BF16)

    cm = proj(COL_C, COL_H) * proj(COL_H, COL_BA)
    scw = scw_ref[...]
    ys = []
    for b in range(nb):
        r0 = b * tile
        scbuf[b, head0:head0 + tile, :] = cm[r0:r0 + tile]
        first = head0 - (SC_CONV - 1)
        acc = scbuf[b, first:first + tile, :] * scw[0:1]
        for i in range(1, SC_CONV):
            acc = acc + scbuf[b, first + i:first + i + tile, :] * scw[i:i + 1]
        tail = scbuf[b, first + tile:head0 + tile, :]
        scnew_ref[b] = tail
        scbuf[b, first:head0, :] = tail
        ys.append(acc)
    sc_out = (proj(COL_B, COL_C) * jnp.concatenate(ys, axis=0)).astype(BF16)

    mix = _mm(og, wo_ref[:GDN_W, :]) + _mm(sc_out, wo_ref[GDN_W:, :])
    y_ref[...] = (x + _rms(mix, gpost_ref[...])).reshape(nb, tile, D_MODEL)


def _ffn_kernel(x_ref, gpre_ref, wg_ref, wu_ref, wd_ref, gpost_ref, y_ref):
    x = x_ref[...]
    h = _rms(x, gpre_ref[...]).astype(BF16)
    f = None
    for lo in range(0, D_FF, FFN_COLS):
        act = (_silu(_mm(h, wg_ref[:, lo:lo + FFN_COLS])) * _mm(h, wu_ref[:, lo:lo + FFN_COLS])).astype(BF16)
        part = _mm(act, wd_ref[lo:lo + FFN_COLS, :])
        f = part if f is None else f + part
    y_ref[...] = x + _rms(f, gpost_ref[...])


def _const_spec(shape):
    return pl.BlockSpec(shape, lambda *_: (0,) * len(shape))


def _mixer(x, conv_prev, s0, sc_prev, p, *, nb, tile, chunk):
    batch, seq, _ = x.shape
    rows = nb * tile
    grid = (batch // nb, seq // tile)
    per_b = lambda i, j: (i, 0, 0)
    in_specs = [
        pl.BlockSpec((nb, tile, D_MODEL), lambda i, j: (i, j, 0)),
        pl.BlockSpec((nb, GDN_CONV - 1, QKV_W), per_b),
        pl.BlockSpec((nb, HEADS, HEAD_DIM, HEAD_DIM), lambda i, j: (i, 0, 0, 0)),
        pl.BlockSpec((nb, SC_CONV - 1, SC_W), per_b),
        _const_spec((1, D_MODEL)),
        _const_spec((D_MODEL, W_IN_COLS)),
        _const_spec((GDN_CONV, QKV_W)),
        _const_spec((1, LANES)),
        _const_spec((1, LANES)),
        _const_spec((1, HEAD_DIM)),
        _const_spec((SC_CONV, SC_W)),
        _const_spec((D_MODEL, D_MODEL)),
        _const_spec((1, D_MODEL)),
    ]
    out_specs = [
        pl.BlockSpec((nb, tile, D_MODEL), lambda i, j: (i, j, 0)),
        pl.BlockSpec((nb, GDN_CONV - 1, QKV_W), per_b),
        pl.BlockSpec((nb, HEADS, HEAD_DIM, HEAD_DIM), lambda i, j: (i, 0, 0, 0)),
        pl.BlockSpec((nb, SC_CONV - 1, SC_W), per_b),
    ]
    out_shape = [
        jax.ShapeDtypeStruct(x.shape, F32),
        jax.ShapeDtypeStruct(conv_prev.shape, F32),
        jax.ShapeDtypeStruct(s0.shape, F32),
        jax.ShapeDtypeStruct(sc_prev.shape, F32),
    ]
    scratch = [
        pltpu.VMEM((nb, HEADS, HEAD_DIM, HEAD_DIM), F32),
        pltpu.VMEM((nb, SUBLANES + tile, QKV_W), F32),
        pltpu.VMEM((nb, SUBLANES + tile, SC_W), F32),
        pltpu.VMEM((rows, GDN_W), F32),
        pltpu.VMEM((rows, GDN_W), F32),
        pltpu.VMEM((rows, GDN_W), F32),
        pltpu.VMEM((rows, LANES), F32),
        pltpu.VMEM((rows, LANES), F32),
        pltpu.VMEM((rows, GDN_W), F32),
    ]
    return pl.pallas_call(
        functools.partial(_mixer_kernel, nb=nb, tile=tile, chunk=chunk),
        grid=grid,
        in_specs=in_specs,
        out_specs=out_specs,
        out_shape=out_shape,
        scratch_shapes=scratch,
        compiler_params=pltpu.CompilerParams(
            dimension_semantics=("arbitrary", "arbitrary"),
            vmem_limit_bytes=VMEM_LIMIT_BYTES),
        name=f"mixer_t{tile}_c{chunk}",
    )(x, conv_prev, s0, sc_prev, p["norm_mix_pre"], p["w_in"], p["conv_qkv_w"], p["a_log"],
      p["dt_bias"], p["gdn_norm_w"], p["conv_sc_w"], p["w_o"], p["norm_mix_post"])


def _ffn(x2d, p, *, tile):
    rows = x2d.shape[0]
    row_spec = pl.BlockSpec((tile, D_MODEL), lambda i: (i, 0))
    return pl.pallas_call(
        _ffn_kernel,
        grid=(rows // tile,),
        in_specs=[
            row_spec,
            _const_spec((1, D_MODEL)),
            _const_spec((D_MODEL, D_FF)),
            _const_spec((D_MODEL, D_FF)),
            _const_spec((D_FF, D_MODEL)),
            _const_spec((1, D_MODEL)),
        ],
        out_specs=row_spec,
        out_shape=jax.ShapeDtypeStruct(x2d.shape, F32),
        compiler_params=pltpu.CompilerParams(
            dimension_semantics=("arbitrary",),
            vmem_limit_bytes=VMEM_LIMIT_BYTES),
        name=f"ffn_t{tile}",
    )(x2d, p["norm_ffn_pre"], p["w_gate"], p["w_up"], p["w_down"], p["norm_ffn_post"])


def _layer_params(l, norm_mix_pre, w_in, conv_qkv_w, a_log, dt_bias, gdn_norm_w, conv_sc_w, w_o,
                  norm_mix_post, norm_ffn_pre, w_gate, w_up, w_down, norm_ffn_post):
    off_z_end = COL_B
    n_ba = 2 * HEADS
    w = w_in[l]
    w_r = jnp.concatenate(
        [w[:, :off_z_end], w[:, off_z_end + n_ba:], w[:, off_z_end:off_z_end + n_ba],
         jnp.zeros((D_MODEL, LANES - n_ba), w.dtype)], axis=1).astype(BF16)
    lane_row = lambda v: jnp.zeros((1, LANES), F32).at[0, A_LANE0:A_LANE0 + HEADS].set(v.astype(F32))
    return {
        "norm_mix_pre": norm_mix_pre[l].reshape(1, D_MODEL),
        "w_in": w_r,
        "conv_qkv_w": conv_qkv_w[l],
        "a_log": lane_row(a_log[l]),
        "dt_bias": lane_row(dt_bias[l]),
        "gdn_norm_w": gdn_norm_w[l].reshape(1, HEAD_DIM),
        "conv_sc_w": conv_sc_w[l],
        "w_o": w_o[l].astype(BF16),
        "norm_mix_post": norm_mix_post[l].reshape(1, D_MODEL),
        "norm_ffn_pre": norm_ffn_pre[l].reshape(1, D_MODEL),
        "w_gate": w_gate[l].astype(BF16),
        "w_up": w_up[l].astype(BF16),
        "w_down": w_down[l].astype(BF16),
        "norm_ffn_post": norm_ffn_post[l].reshape(1, D_MODEL),
    }


def _mixer_tiling(batch, seq):
    chunk = min(CHUNK, seq)
    if seq <= CHUNK:
        return batch, seq, chunk
    return batch, 256, chunk


def _run(x, conv0, s0, sc0, params):
    batch, seq, _ = x.shape
    nb, tile, chunk = _mixer_tiling(batch, seq)
    ffn_tile = min(512, batch * seq)
    convs, states, scs = [], [], []
    for l, p in enumerate(params):
        x, c, s, sc = _mixer(x, conv0[l], s0[l], sc0[l], p, nb=nb, tile=tile, chunk=chunk)
        x = _ffn(x.reshape(batch * seq, D_MODEL), p, tile=ffn_tile).reshape(batch, seq, D_MODEL)
        convs.append(c)
        states.append(s)
        scs.append(sc)
    return x, jnp.stack(convs), jnp.stack(states), jnp.stack(scs)


def kernel(x_prompt, x_sample, cache_gdn_conv, state_gdn, cache_sc_conv, norm_mix_pre, w_in, conv_qkv_w, a_log, dt_bias, gdn_norm_w, conv_sc_w, w_o, norm_mix_post, norm_ffn_pre, w_gate, w_up, w_down, norm_ffn_post):
    depth = w_in.shape[0]
    params = [_layer_params(l, norm_mix_pre, w_in, conv_qkv_w, a_log, dt_bias, gdn_norm_w, conv_sc_w,
                            w_o, norm_mix_post, norm_ffn_pre, w_gate, w_up, w_down, norm_ffn_post)
              for l in range(depth)]
    bp = x_prompt.shape[0]
    zc = jnp.zeros((depth, bp, GDN_CONV - 1, QKV_W), F32)
    zs = jnp.zeros((depth, bp, HEADS, HEAD_DIM, HEAD_DIM), F32)
    zsc = jnp.zeros((depth, bp, SC_CONV - 1, SC_W), F32)
    y_prompt, conv_p, state_p, sc_p = _run(x_prompt, zc, zs, zsc, params)
    y_sample, conv_s, state_s, sc_s = _run(x_sample, cache_gdn_conv, state_gdn, cache_sc_conv, params)
    return (y_prompt, y_sample, conv_p, state_p, sc_p, conv_s, state_s, sc_s)
```

```python
import functools

import jax
import jax.numpy as jnp
from jax import lax
from jax.experimental import pallas as pl
from jax.experimental.pallas import tpu as pltpu

F32 = jnp.float32
BF16 = jnp.bfloat16

D_MODEL = 1024
HEADS = 4
HEAD_DIM = 128
GDN_W = HEADS * HEAD_DIM
SC_W = D_MODEL - GDN_W
QKV_W = 3 * GDN_W
GDN_CONV = 4
SC_CONV = 3
D_FF = 2816
CHUNK = 64
EPS = 1e-6

COL_Z = QKV_W
COL_B = COL_Z + GDN_W
COL_C = COL_B + SC_W
COL_H = COL_C + SC_W
COL_BA = COL_H + SC_W
LANES = 128
SUBLANES = 8
W_IN_COLS = COL_BA + LANES
A_LANE0 = HEADS

VMEM_LIMIT_BYTES = 56 * 1024 * 1024
FFN_COLS = 704


def _rms(x, g):
    return x * lax.rsqrt(jnp.mean(x * x, axis=-1, keepdims=True) + EPS) * g


def _silu(x):
    return x * jax.nn.sigmoid(x)


def _mm(a, b):
    return jnp.dot(a, b, preferred_element_type=F32)


def _mm_nt(a, b):
    return lax.dot_general(a, b, (((1,), (1,)), ((), ())), preferred_element_type=F32)


def _split2(x):
    hi = x.astype(BF16)
    lo = (x - hi.astype(F32)).astype(BF16)
    return hi, lo


def _split3(x):
    h1 = x.astype(BF16)
    r1 = x - h1.astype(F32)
    h2 = r1.astype(BF16)
    h3 = (r1 - h2.astype(F32)).astype(BF16)
    return h1, h2, h3


def _mm3_each(lhs, rhs):
    ls = [_split2(a) for a in lhs]
    rs = [_split2(b) for b in rhs]
    hh = [_mm(l[0], r[0]) for l, r in zip(ls, rs)]
    hl = [_mm(l[0], r[1]) for l, r in zip(ls, rs)]
    lh = [_mm(l[1], r[0]) for l, r in zip(ls, rs)]
    return [a + (b + c) for a, b, c in zip(hh, hl, lh)]


def _unit_lower_inverses(a_strict, row, col, size):
    eye = (row == col).astype(F32)
    ts = [eye - jnp.where((row >> 1) == (col >> 1), a, 0.0) for a in a_strict]
    shift = 1
    while (1 << shift) < size:
        off = ((row >> (shift + 1)) == (col >> (shift + 1))) & ((row >> shift) != (col >> shift))
        inner = _mm3_each([jnp.where(off, a, 0.0) for a in a_strict], ts)
        outer = _mm3_each(ts, inner)
        ts = [t - o for t, o in zip(ts, outer)]
        shift += 1
    return ts


def _mixer_kernel(x_ref, cprev_ref, s0_ref, scprev_ref, gpre_ref, win_ref, cw_ref, alog_ref,
                  dtb_ref, gnw_ref, scw_ref, wo_ref, gpost_ref,
                  y_ref, cnew_ref, snew_ref, scnew_ref,
                  s_scr, cbuf, scbuf, q_scr, k_scr, v_scr, g_scr, beta_scr, o_scr,
                  gc_scr, u_scr, wq_scr, qkk_scr,
                  *, nb, tile, chunk):
    rows = nb * tile
    head0 = SUBLANES

    @pl.when(pl.program_id(1) == 0)
    def _():
        s_scr[...] = s0_ref[...]
        for b in range(nb):
            cbuf[b, head0 - (GDN_CONV - 1):head0, :] = cprev_ref[b]
            scbuf[b, head0 - (SC_CONV - 1):head0, :] = scprev_ref[b]

    x = x_ref[...].reshape(rows, D_MODEL)
    h = _rms(x, gpre_ref[...]).astype(BF16)

    def proj(lo, hi):
        return _mm(h, win_ref[:, lo:hi])

    qkv_in = proj(0, QKV_W)
    cw = cw_ref[...]
    for b in range(nb):
        r0 = b * tile
        cbuf[b, head0:head0 + tile, :] = qkv_in[r0:r0 + tile]
        first = head0 - (GDN_CONV - 1)
        acc = cbuf[b, first:first + tile, :] * cw[0:1]
        for i in range(1, GDN_CONV):
            acc = acc + cbuf[b, first + i:first + i + tile, :] * cw[i:i + 1]
        tail = cbuf[b, first + tile:head0 + tile, :]
        cnew_ref[b] = tail
        cbuf[b, first:head0, :] = tail
        act = _silu(acc)
        for hh in range(HEADS):
            lo = hh * HEAD_DIM
            qh = act[:, lo:lo + HEAD_DIM]
            kh = act[:, GDN_W + lo:GDN_W + lo + HEAD_DIM]
            q_scr[r0:r0 + tile, lo:lo + HEAD_DIM] = (
                qh * lax.rsqrt(jnp.sum(qh * qh, axis=-1, keepdims=True) + EPS) * (HEAD_DIM ** -0.5))
            k_scr[r0:r0 + tile, lo:lo + HEAD_DIM] = (
                kh * lax.rsqrt(jnp.sum(kh * kh, axis=-1, keepdims=True) + EPS))
        v_scr[r0:r0 + tile, :] = act[:, 2 * GDN_W:]

    pba = proj(COL_BA, W_IN_COLS)
    beta_scr[...] = jax.nn.sigmoid(pba)
    ga = pba + dtb_ref[...]
    softplus = jnp.maximum(ga, 0.0) + jnp.log1p(jnp.exp(-jnp.abs(ga)))
    g_scr[...] = -jnp.exp(alog_ref[...]) * softplus

    row = lax.broadcasted_iota(jnp.int32, (chunk, chunk), 0)
    col = lax.broadcasted_iota(jnp.int32, (chunk, chunk), 1)
    causal = row >= col
    strict = row > col
    tri = causal.astype(BF16)
    sel_r = lax.broadcasted_iota(jnp.int32, (SUBLANES, LANES), 0)
    sel_c = lax.broadcasted_iota(jnp.int32, (SUBLANES, LANES), 1)
    sel = (sel_c == sel_r + A_LANE0).astype(BF16)

    probs = [(b, hh) for b in range(nb) for hh in range(HEADS)]
    n_chunks = tile // chunk

    def prep_step(c, carry):
        r0s = [pl.multiple_of(b * tile + c * chunk, chunk) for b in range(nb)]
        g3 = [_split3(g_scr[pl.ds(r0, chunk), :]) for r0 in r0s]
        gcs = [_mm(tri, t[0]) + (_mm(tri, t[1]) + _mm(tri, t[2])) for t in g3]
        gc3 = [_split3(gc) for gc in gcs]
        gcts = [_mm_nt(sel, t[0]) + (_mm_nt(sel, t[1]) + _mm_nt(sel, t[2])) for t in gc3]
        betas = [beta_scr[pl.ds(r0, chunk), :] for r0 in r0s]
        for b in range(nb):
            gc_scr[pl.ds(r0s[b], chunk), :] = gcs[b]
        ops = []
        for b, hh in probs:
            lo = hh * HEAD_DIM
            gcol = gcs[b][:, A_LANE0 + hh:A_LANE0 + hh + 1]
            grow = gcts[b][hh:hh + 1, :]
            bcol = betas[b][:, hh:hh + 1]
            qh = q_scr[pl.ds(r0s[b], chunk), lo:lo + HEAD_DIM]
            kh = k_scr[pl.ds(r0s[b], chunk), lo:lo + HEAD_DIM]
            vh = v_scr[pl.ds(r0s[b], chunk), lo:lo + HEAD_DIM]
            kb = kh * bcol
            eg = jnp.exp(gcol)
            ops.append(dict(
                decay=jnp.exp(jnp.where(causal, gcol - grow, -jnp.inf)),
                q16=qh.astype(BF16), k16=kh.astype(BF16), kb16=kb.astype(BF16),
                rhs=jnp.concatenate([vh * bcol, kb * eg], axis=1),
                qg=qh * eg,
                kt=kh * jnp.exp(gcol[chunk - 1:chunk, :] - gcol)))
        a_mats = [jnp.where(strict, _mm_nt(o["kb16"], o["k16"]) * o["decay"], 0.0) for o in ops]
        qks = [_mm_nt(o["q16"], o["k16"]) * o["decay"] for o in ops]
        sols = _mm3_each(_unit_lower_inverses(a_mats, row, col, chunk), [o["rhs"] for o in ops])
        for (b, hh), o, qk, sol in zip(probs, ops, qks, sols):
            lo = hh * HEAD_DIM
            r0 = r0s[b]
            u_scr[pl.ds(r0, chunk), lo:lo + HEAD_DIM] = sol[:, :HEAD_DIM]
            wq_scr[pl.ds(2 * r0, 2 * chunk), lo:lo + HEAD_DIM] = jnp.concatenate(
                [sol[:, HEAD_DIM:], o["qg"]], axis=0).astype(BF16)
            qkk_scr[(b * n_chunks + c) * HEADS + hh] = jnp.concatenate(
                [qk, o["kt"].T], axis=0).astype(BF16)
        return carry

    def chain_step(c, carry):
        r0s = [pl.multiple_of(b * tile + c * chunk, chunk) for b in range(nb)]
        s_old = [s_scr[b, hh] for b, hh in probs]
        ws = [_mm(wq_scr[pl.ds(2 * r0s[b], 2 * chunk), hh * HEAD_DIM:(hh + 1) * HEAD_DIM], s.astype(BF16))
              for (b, hh), s in zip(probs, s_old)]
        v_new = [(u_scr[pl.ds(r0s[b], chunk), hh * HEAD_DIM:(hh + 1) * HEAD_DIM] - w[:chunk]).astype(BF16)
                 for (b, hh), w in zip(probs, ws)]
        r2 = [_mm(qkk_scr[(b * n_chunks + c) * HEADS + hh], v) for (b, hh), v in zip(probs, v_new)]
        for (b, hh), s, w, r in zip(probs, s_old, ws, r2):
            lo = hh * HEAD_DIM
            o_scr[pl.ds(r0s[b], chunk), lo:lo + HEAD_DIM] = w[chunk:] + r[:chunk]
            glast = gc_scr[pl.ds(r0s[b] + (chunk - 1), 1), :][:, A_LANE0 + hh:A_LANE0 + hh + 1]
            s_scr[b, hh] = s * jnp.exp(glast) + r[chunk:]
        return carry

    lax.fori_loop(0, n_chunks, prep_step, 0)
    lax.fori_loop(0, n_chunks, chain_step, 0)
    snew_ref[...] = s_scr[...]

    z = proj(COL_Z, COL_B)
    gnw = gnw_ref[...]
    og = []
    for hh in range(HEADS):
        lo = hh * HEAD_DIM
        og.append(_rms(o_scr[:, lo:lo + HEAD_DIM], gnw) * _silu(z[:, lo:lo + HEAD_DIM]))
    og = jnp.concatenate(og, axis=1).astype(BF16)

    cm = proj(COL_C, COL_H) * proj(COL_H, COL_BA)
    scw = scw_ref[...]
    ys = []
    for b in range(nb):
        r0 = b * tile
        scbuf[b, head0:head0 + tile, :] = cm[r0:r0 + tile]
        first = head0 - (SC_CONV - 1)
        acc = scbuf[b, first:first + tile, :] * scw[0:1]
        for i in range(1, SC_CONV):
            acc = acc + scbuf[b, first + i:first + i + tile, :] * scw[i:i + 1]
        tail = scbuf[b, first + tile:head0 + tile, :]
        scnew_ref[b] = tail
        scbuf[b, first:head0, :] = tail
        ys.append(acc)
    sc_out = (proj(COL_B, COL_C) * jnp.concatenate(ys, axis=0)).astype(BF16)

    mix = _mm(og, wo_ref[:GDN_W, :]) + _mm(sc_out, wo_ref[GDN_W:, :])
    y_ref[...] = (x + _rms(mix, gpost_ref[...])).reshape(nb, tile, D_MODEL)


def _ffn_kernel(x_ref, gpre_ref, wg_ref, wu_ref, wd_ref, gpost_ref, y_ref):
    x = x_ref[...]
    h = _rms(x, gpre_ref[...]).astype(BF16)
    f = None
    for lo in range(0, D_FF, FFN_COLS):
        act = (_silu(_mm(h, wg_ref[:, lo:lo + FFN_COLS])) * _mm(h, wu_ref[:, lo:lo + FFN_COLS])).astype(BF16)
        part = _mm(act, wd_ref[lo:lo + FFN_COLS, :])
        f = part if f is None else f + part
    y_ref[...] = x + _rms(f, gpost_ref[...])


def _const_spec(shape):
    return pl.BlockSpec(shape, lambda *_: (0,) * len(shape))


def _mixer(x, conv_prev, s0, sc_prev, p, *, nb, tile, chunk):
    batch, seq, _ = x.shape
    rows = nb * tile
    grid = (batch // nb, seq // tile)
    per_b = lambda i, j: (i, 0, 0)
    in_specs = [
        pl.BlockSpec((nb, tile, D_MODEL), lambda i, j: (i, j, 0)),
        pl.BlockSpec((nb, GDN_CONV - 1, QKV_W), per_b),
        pl.BlockSpec((nb, HEADS, HEAD_DIM, HEAD_DIM), lambda i, j: (i, 0, 0, 0)),
        pl.BlockSpec((nb, SC_CONV - 1, SC_W), per_b),
        _const_spec((1, D_MODEL)),
        _const_spec((D_MODEL, W_IN_COLS)),
        _const_spec((GDN_CONV, QKV_W)),
        _const_spec((1, LANES)),
        _const_spec((1, LANES)),
        _const_spec((1, HEAD_DIM)),
        _const_spec((SC_CONV, SC_W)),
        _const_spec((D_MODEL, D_MODEL)),
        _const_spec((1, D_MODEL)),
    ]
    out_specs = [
        pl.BlockSpec((nb, tile, D_MODEL), lambda i, j: (i, j, 0)),
        pl.BlockSpec((nb, GDN_CONV - 1, QKV_W), per_b),
        pl.BlockSpec((nb, HEADS, HEAD_DIM, HEAD_DIM), lambda i, j: (i, 0, 0, 0)),
        pl.BlockSpec((nb, SC_CONV - 1, SC_W), per_b),
    ]
    out_shape = [
        jax.ShapeDtypeStruct(x.shape, F32),
        jax.ShapeDtypeStruct(conv_prev.shape, F32),
        jax.ShapeDtypeStruct(s0.shape, F32),
        jax.ShapeDtypeStruct(sc_prev.shape, F32),
    ]
    scratch = [
        pltpu.VMEM((nb, HEADS, HEAD_DIM, HEAD_DIM), F32),
        pltpu.VMEM((nb, SUBLANES + tile, QKV_W), F32),
        pltpu.VMEM((nb, SUBLANES + tile, SC_W), F32),
        pltpu.VMEM((rows, GDN_W), F32),
        pltpu.VMEM((rows, GDN_W), F32),
        pltpu.VMEM((rows, GDN_W), F32),
        pltpu.VMEM((rows, LANES), F32),
        pltpu.VMEM((rows, LANES), F32),
        pltpu.VMEM((rows, GDN_W), F32),
        pltpu.VMEM((rows, LANES), F32),
        pltpu.VMEM((rows, GDN_W), F32),
        pltpu.VMEM((2 * rows, GDN_W), BF16),
        pltpu.VMEM((rows // chunk * HEADS, chunk + HEAD_DIM, chunk), BF16),
    ]
    return pl.pallas_call(
        functools.partial(_mixer_kernel, nb=nb, tile=tile, chunk=chunk),
        grid=grid,
        in_specs=in_specs,
        out_specs=out_specs,
        out_shape=out_shape,
        scratch_shapes=scratch,
        compiler_params=pltpu.CompilerParams(
            dimension_semantics=("arbitrary", "arbitrary"),
            vmem_limit_bytes=VMEM_LIMIT_BYTES),
        name=f"mixer_t{tile}_c{chunk}",
    )(x, conv_prev, s0, sc_prev, p["norm_mix_pre"], p["w_in"], p["conv_qkv_w"], p["a_log"],
      p["dt_bias"], p["gdn_norm_w"], p["conv_sc_w"], p["w_o"], p["norm_mix_post"])


def _ffn(x2d, p, *, tile):
    rows = x2d.shape[0]
    row_spec = pl.BlockSpec((tile, D_MODEL), lambda i: (i, 0))
    return pl.pallas_call(
        _ffn_kernel,
        grid=(rows // tile,),
        in_specs=[
            row_spec,
            _const_spec((1, D_MODEL)),
            _const_spec((D_MODEL, D_FF)),
            _const_spec((D_MODEL, D_FF)),
            _const_spec((D_FF, D_MODEL)),
            _const_spec((1, D_MODEL)),
        ],
        out_specs=row_spec,
        out_shape=jax.ShapeDtypeStruct(x2d.shape, F32),
        compiler_params=pltpu.CompilerParams(
            dimension_semantics=("arbitrary",),
            vmem_limit_bytes=VMEM_LIMIT_BYTES),
        name=f"ffn_t{tile}",
    )(x2d, p["norm_ffn_pre"], p["w_gate"], p["w_up"], p["w_down"], p["norm_ffn_post"])


def _layer_params(l, norm_mix_pre, w_in, conv_qkv_w, a_log, dt_bias, gdn_norm_w, conv_sc_w, w_o,
                  norm_mix_post, norm_ffn_pre, w_gate, w_up, w_down, norm_ffn_post):
    off_z_end = COL_B
    n_ba = 2 * HEADS
    w = w_in[l]
    w_r = jnp.concatenate(
        [w[:, :off_z_end], w[:, off_z_end + n_ba:], w[:, off_z_end:off_z_end + n_ba],
         jnp.zeros((D_MODEL, LANES - n_ba), w.dtype)], axis=1).astype(BF16)
    lane_row = lambda v: jnp.zeros((1, LANES), F32).at[0, A_LANE0:A_LANE0 + HEADS].set(v.astype(F32))
    return {
        "norm_mix_pre": norm_mix_pre[l].reshape(1, D_MODEL),
        "w_in": w_r,
        "conv_qkv_w": conv_qkv_w[l],
        "a_log": lane_row(a_log[l]),
        "dt_bias": lane_row(dt_bias[l]),
        "gdn_norm_w": gdn_norm_w[l].reshape(1, HEAD_DIM),
        "conv_sc_w": conv_sc_w[l],
        "w_o": w_o[l].astype(BF16),
        "norm_mix_post": norm_mix_post[l].reshape(1, D_MODEL),
        "norm_ffn_pre": norm_ffn_pre[l].reshape(1, D_MODEL),
        "w_gate": w_gate[l].astype(BF16),
        "w_up": w_up[l].astype(BF16),
        "w_down": w_down[l].astype(BF16),
        "norm_ffn_post": norm_ffn_post[l].reshape(1, D_MODEL),
    }


def _mixer_tiling(batch, seq):
    chunk = min(CHUNK, seq)
    if seq <= CHUNK:
        return batch, seq, chunk
    return batch, 256, chunk


def _run(x, conv0, s0, sc0, params):
    batch, seq, _ = x.shape
    nb, tile, chunk = _mixer_tiling(batch, seq)
    ffn_tile = min(512, batch * seq)
    convs, states, scs = [], [], []
    for l, p in enumerate(params):
        x, c, s, sc = _mixer(x, conv0[l], s0[l], sc0[l], p, nb=nb, tile=tile, chunk=chunk)
        x = _ffn(x.reshape(batch * seq, D_MODEL), p, tile=ffn_tile).reshape(batch, seq, D_MODEL)
        convs.append(c)
        states.append(s)
        scs.append(sc)
    return x, jnp.stack(convs), jnp.stack(states), jnp.stack(scs)


def kernel(x_prompt, x_sample, cache_gdn_conv, state_gdn, cache_sc_conv, norm_mix_pre, w_in, conv_qkv_w, a_log, dt_bias, gdn_norm_w, conv_sc_w, w_o, norm_mix_post, norm_ffn_pre, w_gate, w_up, w_down, norm_ffn_post):
    depth = w_in.shape[0]
    params = [_layer_params(l, norm_mix_pre, w_in, conv_qkv_w, a_log, dt_bias, gdn_norm_w, conv_sc_w,
                            w_o, norm_mix_post, norm_ffn_pre, w_gate, w_up, w_down, norm_ffn_post)
              for l in range(depth)]
    bp = x_prompt.shape[0]
    zc = jnp.zeros((depth, bp, GDN_CONV - 1, QKV_W), F32)
    zs = jnp.zeros((depth, bp, HEADS, HEAD_DIM, HEAD_DIM), F32)
    zsc = jnp.zeros((depth, bp, SC_CONV - 1, SC_W), F32)
    y_prompt, conv_p, state_p, sc_p = _run(x_prompt, zc, zs, zsc, params)
    y_sample, conv_s, state_s, sc_s = _run(x_sample, cache_gdn_conv, state_gdn, cache_sc_conv, params)
    return (y_prompt, y_sample, conv_p, state_p, sc_p, conv_s, state_s, sc_s)
```

```python
import functools

import jax
import jax.numpy as jnp
from jax import lax
from jax.experimental import pallas as pl
from jax.experimental.pallas import tpu as pltpu

F32 = jnp.float32
BF16 = jnp.bfloat16

D_MODEL = 1024
HEADS = 4
HEAD_DIM = 128
GDN_W = HEADS * HEAD_DIM
SC_W = D_MODEL - GDN_W
QKV_W = 3 * GDN_W
GDN_CONV = 4
SC_CONV = 3
D_FF = 2816
CHUNK = 64
EPS = 1e-6

COL_Z = QKV_W
COL_B = COL_Z + GDN_W
COL_C = COL_B + SC_W
COL_H = COL_C + SC_W
COL_BA = COL_H + SC_W
LANES = 128
SUBLANES = 8
W_IN_COLS = COL_BA + LANES
A_LANE0 = HEADS

VMEM_LIMIT_BYTES = 56 * 1024 * 1024
FFN_COLS = 704
PREP_CHUNKS = 4
MAX_UNROLLED_TRIPS = 4


def _rms(x, g):
    return x * lax.rsqrt(jnp.mean(x * x, axis=-1, keepdims=True) + EPS) * g


def _silu(x):
    return x * jax.nn.sigmoid(x)


def _mm(a, b):
    return jnp.dot(a, b, preferred_element_type=F32)


def _mm_nt(a, b):
    return lax.dot_general(a, b, (((1,), (1,)), ((), ())), preferred_element_type=F32)


def _aligned(index, multiple):
    return index if isinstance(index, int) else pl.multiple_of(index, multiple)


def _loop(trips, body):
    if trips <= MAX_UNROLLED_TRIPS:
        for i in range(trips):
            body(i, 0)
    else:
        lax.fori_loop(0, trips, body, 0)


def _split2(x):
    hi = x.astype(BF16)
    lo = (x - hi.astype(F32)).astype(BF16)
    return hi, lo


def _split3(x):
    h1 = x.astype(BF16)
    r1 = x - h1.astype(F32)
    h2 = r1.astype(BF16)
    h3 = (r1 - h2.astype(F32)).astype(BF16)
    return h1, h2, h3


def _mm3_each(lhs, rhs):
    ls = [_split2(a) for a in lhs]
    rs = [_split2(b) for b in rhs]
    hh = [_mm(l[0], r[0]) for l, r in zip(ls, rs)]
    hl = [_mm(l[0], r[1]) for l, r in zip(ls, rs)]
    lh = [_mm(l[1], r[0]) for l, r in zip(ls, rs)]
    return [a + (b + c) for a, b, c in zip(hh, hl, lh)]


def _unit_lower_inverses(a_strict, row, col, size):
    eye = (row == col).astype(F32)
    ts = [eye - jnp.where((row >> 1) == (col >> 1), a, 0.0) for a in a_strict]
    a16 = [a.astype(BF16) for a in a_strict]
    shift = 1
    while (1 << shift) < size:
        off = ((row >> (shift + 1)) == (col >> (shift + 1))) & ((row >> shift) != (col >> shift))
        t16 = [t.astype(BF16) for t in ts]
        inner = [_mm(jnp.where(off, a, jnp.zeros_like(a)), t) for a, t in zip(a16, t16)]
        outer = [_mm(t, i.astype(BF16)) for t, i in zip(t16, inner)]
        ts = [t - o for t, o in zip(ts, outer)]
        shift += 1
    return ts


def _solve_refined(a_strict, t_approx, rhs):
    t16 = [t.astype(BF16) for t in t_approx]
    x0 = [_mm(t, r.astype(BF16)) for t, r in zip(t16, rhs)]
    ax0 = _mm3_each(a_strict, x0)
    res = [r - x - ax for r, x, ax in zip(rhs, x0, ax0)]
    corr = [_mm(t, r.astype(BF16)) for t, r in zip(t16, res)]
    return [x + c for x, c in zip(x0, corr)]


def _mixer_kernel(x_ref, cprev_ref, s0_ref, scprev_ref, gpre_ref, win_ref, cw_ref, alog_ref,
                  dtb_ref, gnw_ref, scw_ref, wo_ref, gpost_ref,
                  y_ref, cnew_ref, snew_ref, scnew_ref,
                  s_scr, cbuf, scbuf, q_scr, k_scr, v_scr, g_scr, beta_scr, o_scr,
                  gc_scr, u_scr, wq_scr, qkk_scr, zg_scr, sc_scr,
                  *, nb, tile, chunk):
    rows = nb * tile
    head0 = SUBLANES

    @pl.when(pl.program_id(1) == 0)
    def _():
        s_scr[...] = s0_ref[...]
        for b in range(nb):
            cbuf[b, head0 - (GDN_CONV - 1):head0, :] = cprev_ref[b]
            scbuf[b, head0 - (SC_CONV - 1):head0, :] = scprev_ref[b]

    x = x_ref[...].reshape(rows, D_MODEL)
    h = _rms(x, gpre_ref[...]).astype(BF16)

    def proj(lo, hi):
        return _mm(h, win_ref[:, lo:hi])

    qkv_in = proj(0, QKV_W)
    pba = proj(COL_BA, W_IN_COLS)
    z_in = proj(COL_Z, COL_B)
    sc_c = proj(COL_C, COL_H)
    sc_h = proj(COL_H, COL_BA)
    sc_b = proj(COL_B, COL_C)

    cw = cw_ref[...]
    for b in range(nb):
        r0 = b * tile
        cbuf[b, head0:head0 + tile, :] = qkv_in[r0:r0 + tile]
        first = head0 - (GDN_CONV - 1)
        acc = cbuf[b, first:first + tile, :] * cw[0:1]
        for i in range(1, GDN_CONV):
            acc = acc + cbuf[b, first + i:first + i + tile, :] * cw[i:i + 1]
        tail = cbuf[b, first + tile:head0 + tile, :]
        cnew_ref[b] = tail
        cbuf[b, first:head0, :] = tail
        act = _silu(acc)
        for hh in range(HEADS):
            lo = hh * HEAD_DIM
            qh = act[:, lo:lo + HEAD_DIM]
            kh = act[:, GDN_W + lo:GDN_W + lo + HEAD_DIM]
            q_scr[r0:r0 + tile, lo:lo + HEAD_DIM] = (
                qh * lax.rsqrt(jnp.sum(qh * qh, axis=-1, keepdims=True) + EPS) * (HEAD_DIM ** -0.5))
            k_scr[r0:r0 + tile, lo:lo + HEAD_DIM] = (
                kh * lax.rsqrt(jnp.sum(kh * kh, axis=-1, keepdims=True) + EPS))
        v_scr[r0:r0 + tile, :] = act[:, 2 * GDN_W:]

    beta_scr[...] = jax.nn.sigmoid(pba)
    ga = pba + dtb_ref[...]
    softplus = jnp.maximum(ga, 0.0) + jnp.log1p(jnp.exp(-jnp.abs(ga)))
    g_scr[...] = -jnp.exp(alog_ref[...]) * softplus

    row = lax.broadcasted_iota(jnp.int32, (chunk, chunk), 0)
    col = lax.broadcasted_iota(jnp.int32, (chunk, chunk), 1)
    causal = row >= col
    strict = row > col
    tri = causal.astype(BF16)
    sel_r = lax.broadcasted_iota(jnp.int32, (SUBLANES, LANES), 0)
    sel_c = lax.broadcasted_iota(jnp.int32, (SUBLANES, LANES), 1)
    sel = (sel_c == sel_r + A_LANE0).astype(BF16)

    zg_scr[...] = _silu(z_in)
    cm = sc_c * sc_h
    scw = scw_ref[...]
    for b in range(nb):
        r0 = b * tile
        scbuf[b, head0:head0 + tile, :] = cm[r0:r0 + tile]
        first = head0 - (SC_CONV - 1)
        acc = scbuf[b, first:first + tile, :] * scw[0:1]
        for i in range(1, SC_CONV):
            acc = acc + scbuf[b, first + i:first + i + tile, :] * scw[i:i + 1]
        tail = scbuf[b, first + tile:head0 + tile, :]
        scnew_ref[b] = tail
        scbuf[b, first:head0, :] = tail
        sc_scr[r0:r0 + tile, :] = acc
    sc_scr[...] = sc_b * sc_scr[...]

    n_chunks = tile // chunk
    cpi = min(PREP_CHUNKS, n_chunks)
    units = [(b, k) for b in range(nb) for k in range(cpi)]
    prep_probs = [(ui, hh) for ui in range(len(units)) for hh in range(HEADS)]
    chain_probs = [(b, hh) for b in range(nb) for hh in range(HEADS)]

    def prep_step(it, carry):
        cidx = [it * cpi + k for _, k in units]
        r0s = [_aligned(b * tile + ci * chunk, chunk) for (b, _), ci in zip(units, cidx)]
        g3 = [_split3(g_scr[pl.ds(r0, chunk), :]) for r0 in r0s]
        gcs = [_mm(tri, t[0]) + (_mm(tri, t[1]) + _mm(tri, t[2])) for t in g3]
        gc3 = [_split3(gc) for gc in gcs]
        gcts = [_mm_nt(sel, t[0]) + (_mm_nt(sel, t[1]) + _mm_nt(sel, t[2])) for t in gc3]
        betas = [beta_scr[pl.ds(r0, chunk), :] for r0 in r0s]
        for r0, gc in zip(r0s, gcs):
            gc_scr[pl.ds(r0, chunk), :] = gc
        ops = []
        for ui, hh in prep_probs:
            lo = hh * HEAD_DIM
            gcol = gcs[ui][:, A_LANE0 + hh:A_LANE0 + hh + 1]
            grow = gcts[ui][hh:hh + 1, :]
            bcol = betas[ui][:, hh:hh + 1]
            qh = q_scr[pl.ds(r0s[ui], chunk), lo:lo + HEAD_DIM]
            kh = k_scr[pl.ds(r0s[ui], chunk), lo:lo + HEAD_DIM]
            vh = v_scr[pl.ds(r0s[ui], chunk), lo:lo + HEAD_DIM]
            kb = kh * bcol
            eg = jnp.exp(gcol)
            ops.append(dict(
                decay=jnp.exp(jnp.where(causal, gcol - grow, -jnp.inf)),
                q16=qh.astype(BF16), k16=kh.astype(BF16), kb16=kb.astype(BF16),
                rhs=jnp.concatenate([vh * bcol, kb * eg], axis=1),
                qg=qh * eg,
                kt=kh * jnp.exp(gcol[chunk - 1:chunk, :] - gcol)))
        a_mats = [jnp.where(strict, _mm_nt(o["kb16"], o["k16"]) * o["decay"], 0.0) for o in ops]
        qks = [_mm_nt(o["q16"], o["k16"]) * o["decay"] for o in ops]
        sols = _solve_refined(a_mats, _unit_lower_inverses(a_mats, row, col, chunk), [o["rhs"] for o in ops])
        for (ui, hh), o, qk, sol in zip(prep_probs, ops, qks, sols):
            lo = hh * HEAD_DIM
            r0 = r0s[ui]
            u_scr[pl.ds(r0, chunk), lo:lo + HEAD_DIM] = sol[:, :HEAD_DIM]
            wq_scr[pl.ds(2 * r0, 2 * chunk), lo:lo + HEAD_DIM] = jnp.concatenate(
                [sol[:, HEAD_DIM:], o["qg"]], axis=0).astype(BF16)
            qkk_scr[(units[ui][0] * n_chunks + cidx[ui]) * HEADS + hh] = jnp.concatenate(
                [qk, o["kt"].T], axis=0).astype(BF16)
        return carry

    def chain_step(c, carry):
        r0s = [_aligned(b * tile + c * chunk, chunk) for b in range(nb)]
        s_old = [s_scr[b, hh] for b, hh in chain_probs]
        ws = [_mm(wq_scr[pl.ds(2 * r0s[b], 2 * chunk), hh * HEAD_DIM:(hh + 1) * HEAD_DIM], s.astype(BF16))
              for (b, hh), s in zip(chain_probs, s_old)]
        v_new = [(u_scr[pl.ds(r0s[b], chunk), hh * HEAD_DIM:(hh + 1) * HEAD_DIM] - w[:chunk]).astype(BF16)
                 for (b, hh), w in zip(chain_probs, ws)]
        r2 = [_mm(qkk_scr[(b * n_chunks + c) * HEADS + hh], v) for (b, hh), v in zip(chain_probs, v_new)]
        for (b, hh), s, w, r in zip(chain_probs, s_old, ws, r2):
            lo = hh * HEAD_DIM
            o_scr[pl.ds(r0s[b], chunk), lo:lo + HEAD_DIM] = w[chunk:] + r[:chunk]
            glast = gc_scr[pl.ds(r0s[b] + (chunk - 1), 1), :][:, A_LANE0 + hh:A_LANE0 + hh + 1]
            s_scr[b, hh] = s * jnp.exp(glast) + r[chunk:]
        return carry

    _loop(n_chunks // cpi, prep_step)
    _loop(n_chunks, chain_step)
    snew_ref[...] = s_scr[...]

    gnw = gnw_ref[...]
    og = []
    for hh in range(HEADS):
        lo = hh * HEAD_DIM
        og.append(_rms(o_scr[:, lo:lo + HEAD_DIM], gnw) * zg_scr[:, lo:lo + HEAD_DIM])
    og = jnp.concatenate(og, axis=1).astype(BF16)
    mix = _mm(og, wo_ref[:GDN_W, :]) + _mm(sc_scr[...].astype(BF16), wo_ref[GDN_W:, :])
    y_ref[...] = (x_ref[...].reshape(rows, D_MODEL) + _rms(mix, gpost_ref[...])).reshape(nb, tile, D_MODEL)


def _ffn_kernel(x_ref, gpre_ref, wg_ref, wu_ref, wd_ref, gpost_ref, y_ref):
    x = x_ref[...]
    h = _rms(x, gpre_ref[...]).astype(BF16)
    f = None
    for lo in range(0, D_FF, FFN_COLS):
        act = (_silu(_mm(h, wg_ref[:, lo:lo + FFN_COLS])) * _mm(h, wu_ref[:, lo:lo + FFN_COLS])).astype(BF16)
        part = _mm(act, wd_ref[lo:lo + FFN_COLS, :])
        f = part if f is None else f + part
    y_ref[...] = x + _rms(f, gpost_ref[...])


def _const_spec(shape):
    return pl.BlockSpec(shape, lambda *_: (0,) * len(shape))


def _mixer(x, conv_prev, s0, sc_prev, p, *, nb, tile, chunk):
    batch, seq, _ = x.shape
    rows = nb * tile
    grid = (batch // nb, seq // tile)
    per_b = lambda i, j: (i, 0, 0)
    in_specs = [
        pl.BlockSpec((nb, tile, D_MODEL), lambda i, j: (i, j, 0)),
        pl.BlockSpec((nb, GDN_CONV - 1, QKV_W), per_b),
        pl.BlockSpec((nb, HEADS, HEAD_DIM, HEAD_DIM), lambda i, j: (i, 0, 0, 0)),
        pl.BlockSpec((nb, SC_CONV - 1, SC_W), per_b),
        _const_spec((1, D_MODEL)),
        _const_spec((D_MODEL, W_IN_COLS)),
        _const_spec((GDN_CONV, QKV_W)),
        _const_spec((1, LANES)),
        _const_spec((1, LANES)),
        _const_spec((1, HEAD_DIM)),
        _const_spec((SC_CONV, SC_W)),
        _const_spec((D_MODEL, D_MODEL)),
        _const_spec((1, D_MODEL)),
    ]
    out_specs = [
        pl.BlockSpec((nb, tile, D_MODEL), lambda i, j: (i, j, 0)),
        pl.BlockSpec((nb, GDN_CONV - 1, QKV_W), per_b),
        pl.BlockSpec((nb, HEADS, HEAD_DIM, HEAD_DIM), lambda i, j: (i, 0, 0, 0)),
        pl.BlockSpec((nb, SC_CONV - 1, SC_W), per_b),
    ]
    out_shape = [
        jax.ShapeDtypeStruct(x.shape, F32),
        jax.ShapeDtypeStruct(conv_prev.shape, F32),
        jax.ShapeDtypeStruct(s0.shape, F32),
        jax.ShapeDtypeStruct(sc_prev.shape, F32),
    ]
    scratch = [
        pltpu.VMEM((nb, HEADS, HEAD_DIM, HEAD_DIM), F32),
        pltpu.VMEM((nb, SUBLANES + tile, QKV_W), F32),
        pltpu.VMEM((nb, SUBLANES + tile, SC_W), F32),
        pltpu.VMEM((rows, GDN_W), F32),
        pltpu.VMEM((rows, GDN_W), F32),
        pltpu.VMEM((rows, GDN_W), F32),
        pltpu.VMEM((rows, LANES), F32),
        pltpu.VMEM((rows, LANES), F32),
        pltpu.VMEM((rows, GDN_W), F32),
        pltpu.VMEM((rows, LANES), F32),
        pltpu.VMEM((rows, GDN_W), F32),
        pltpu.VMEM((2 * rows, GDN_W), BF16),
        pltpu.VMEM((rows // chunk * HEADS, chunk + HEAD_DIM, chunk), BF16),
        pltpu.VMEM((rows, GDN_W), F32),
        pltpu.VMEM((rows, SC_W), F32),
    ]
    return pl.pallas_call(
        functools.partial(_mixer_kernel, nb=nb, tile=tile, chunk=chunk),
        grid=grid,
        in_specs=in_specs,
        out_specs=out_specs,
        out_shape=out_shape,
        scratch_shapes=scratch,
        compiler_params=pltpu.CompilerParams(
            dimension_semantics=("arbitrary", "arbitrary"),
            vmem_limit_bytes=VMEM_LIMIT_BYTES),
        name=f"mixer_t{tile}_c{chunk}",
    )(x, conv_prev, s0, sc_prev, p["norm_mix_pre"], p["w_in"], p["conv_qkv_w"], p["a_log"],
      p["dt_bias"], p["gdn_norm_w"], p["conv_sc_w"], p["w_o"], p["norm_mix_post"])


def _ffn(x2d, p, *, tile):
    rows = x2d.shape[0]
    row_spec = pl.BlockSpec((tile, D_MODEL), lambda i: (i, 0))
    return pl.pallas_call(
        _ffn_kernel,
        grid=(rows // tile,),
        in_specs=[
            row_spec,
            _const_spec((1, D_MODEL)),
            _const_spec((D_MODEL, D_FF)),
            _const_spec((D_MODEL, D_FF)),
            _const_spec((D_FF, D_MODEL)),
            _const_spec((1, D_MODEL)),
        ],
        out_specs=row_spec,
        out_shape=jax.ShapeDtypeStruct(x2d.shape, F32),
        compiler_params=pltpu.CompilerParams(
            dimension_semantics=("arbitrary",),
            vmem_limit_bytes=VMEM_LIMIT_BYTES),
        name=f"ffn_t{tile}",
    )(x2d, p["norm_ffn_pre"], p["w_gate"], p["w_up"], p["w_down"], p["norm_ffn_post"])


def _layer_params(l, norm_mix_pre, w_in, conv_qkv_w, a_log, dt_bias, gdn_norm_w, conv_sc_w, w_o,
                  norm_mix_post, norm_ffn_pre, w_gate, w_up, w_down, norm_ffn_post):
    off_z_end = COL_B
    n_ba = 2 * HEADS
    w = w_in[l]
    w_r = jnp.concatenate(
        [w[:, :off_z_end], w[:, off_z_end + n_ba:], w[:, off_z_end:off_z_end + n_ba],
         jnp.zeros((D_MODEL, LANES - n_ba), w.dtype)], axis=1).astype(BF16)
    lane_row = lambda v: jnp.zeros((1, LANES), F32).at[0, A_LANE0:A_LANE0 + HEADS].set(v.astype(F32))
    return {
        "norm_mix_pre": norm_mix_pre[l].reshape(1, D_MODEL),
        "w_in": w_r,
        "conv_qkv_w": conv_qkv_w[l],
        "a_log": lane_row(a_log[l]),
        "dt_bias": lane_row(dt_bias[l]),
        "gdn_norm_w": gdn_norm_w[l].reshape(1, HEAD_DIM),
        "conv_sc_w": conv_sc_w[l],
        "w_o": w_o[l].astype(BF16),
        "norm_mix_post": norm_mix_post[l].reshape(1, D_MODEL),
        "norm_ffn_pre": norm_ffn_pre[l].reshape(1, D_MODEL),
        "w_gate": w_gate[l].astype(BF16),
        "w_up": w_up[l].astype(BF16),
        "w_down": w_down[l].astype(BF16),
        "norm_ffn_post": norm_ffn_post[l].reshape(1, D_MODEL),
    }


def _mixer_tiling(batch, seq):
    chunk = min(CHUNK, seq)
    if seq <= CHUNK:
        return batch, seq, chunk
    return batch, 256, chunk


def _run(x, conv0, s0, sc0, params):
    batch, seq, _ = x.shape
    nb, tile, chunk = _mixer_tiling(batch, seq)
    ffn_tile = min(512, batch * seq)
    convs, states, scs = [], [], []
    for l, p in enumerate(params):
        x, c, s, sc = _mixer(x, conv0[l], s0[l], sc0[l], p, nb=nb, tile=tile, chunk=chunk)
        x = _ffn(x.reshape(batch * seq, D_MODEL), p, tile=ffn_tile).reshape(batch, seq, D_MODEL)
        convs.append(c)
        states.append(s)
        scs.append(sc)
    return x, jnp.stack(convs), jnp.stack(states), jnp.stack(scs)


def kernel(x_prompt, x_sample, cache_gdn_conv, state_gdn, cache_sc_conv, norm_mix_pre, w_in, conv_qkv_w, a_log, dt_bias, gdn_norm_w, conv_sc_w, w_o, norm_mix_post, norm_ffn_pre, w_gate, w_up, w_down, norm_ffn_post):
    depth = w_in.shape[0]
    params = [_layer_params(l, norm_mix_pre, w_in, conv_qkv_w, a_log, dt_bias, gdn_norm_w, conv_sc_w,
                            w_o, norm_mix_post, norm_ffn_pre, w_gate, w_up, w_down, norm_ffn_post)
              for l in range(depth)]
    bp = x_prompt.shape[0]
    zc = jnp.zeros((depth, bp, GDN_CONV - 1, QKV_W), F32)
    zs = jnp.zeros((depth, bp, HEADS, HEAD_DIM, HEAD_DIM), F32)
    zsc = jnp.zeros((depth, bp, SC_CONV - 1, SC_W), F32)
    y_prompt, conv_p, state_p, sc_p = _run(x_prompt, zc, zs, zsc, params)
    y_sample, conv_s, state_s, sc_s = _run(x_sample, cache_gdn_conv, state_gdn, cache_sc_conv, params)
    return (y_prompt, y_sample, conv_p, state_p, sc_p, conv_s, state_s, sc_s)
```

```python
import functools

import jax
import jax.numpy as jnp
from jax import lax
from jax.experimental import pallas as pl
from jax.experimental.pallas import tpu as pltpu

F32 = jnp.float32
BF16 = jnp.bfloat16

D_MODEL = 1024
HEADS = 4
HEAD_DIM = 128
GDN_W = HEADS * HEAD_DIM
SC_W = D_MODEL - GDN_W
QKV_W = 3 * GDN_W
GDN_CONV = 4
SC_CONV = 3
D_FF = 2816
CHUNK = 64
EPS = 1e-6

COL_Z = QKV_W
COL_B = COL_Z + GDN_W
COL_C = COL_B + SC_W
COL_H = COL_C + SC_W
COL_BA = COL_H + SC_W
LANES = 128
SUBLANES = 8
W_IN_COLS = COL_BA + LANES
A_LANE0 = HEADS

VMEM_LIMIT_BYTES = 56 * 1024 * 1024
FFN_COLS = 704
FFN_TILE = 1024
PREP_CHUNKS = 4
MAX_UNROLLED_TRIPS = 4


def _rms(x, g):
    return x * lax.rsqrt(jnp.mean(x * x, axis=-1, keepdims=True) + EPS) * g


def _silu(x):
    return x * jax.nn.sigmoid(x)


def _mm(a, b):
    return jnp.dot(a, b, preferred_element_type=F32)


def _mm_nt(a, b):
    return lax.dot_general(a, b, (((1,), (1,)), ((), ())), preferred_element_type=F32)


def _aligned(index, multiple):
    return index if isinstance(index, int) else pl.multiple_of(index, multiple)


def _loop(trips, body):
    if trips <= MAX_UNROLLED_TRIPS:
        for i in range(trips):
            body(i, 0)
    else:
        lax.fori_loop(0, trips, body, 0)


def _split2(x):
    hi = x.astype(BF16)
    lo = (x - hi.astype(F32)).astype(BF16)
    return hi, lo


def _split3(x):
    h1 = x.astype(BF16)
    r1 = x - h1.astype(F32)
    h2 = r1.astype(BF16)
    h3 = (r1 - h2.astype(F32)).astype(BF16)
    return h1, h2, h3


def _mm3_each(lhs, rhs):
    ls = [_split2(a) for a in lhs]
    rs = [_split2(b) for b in rhs]
    hh = [_mm(l[0], r[0]) for l, r in zip(ls, rs)]
    hl = [_mm(l[0], r[1]) for l, r in zip(ls, rs)]
    lh = [_mm(l[1], r[0]) for l, r in zip(ls, rs)]
    return [a + (b + c) for a, b, c in zip(hh, hl, lh)]


def _unit_lower_inverses(a_strict, row, col, size):
    eye = (row == col).astype(F32)
    ts = [eye - jnp.where((row >> 1) == (col >> 1), a, 0.0) for a in a_strict]
    a16 = [a.astype(BF16) for a in a_strict]
    shift = 1
    while (1 << shift) < size:
        off = ((row >> (shift + 1)) == (col >> (shift + 1))) & ((row >> shift) != (col >> shift))
        t16 = [t.astype(BF16) for t in ts]
        inner = [_mm(jnp.where(off, a, jnp.zeros_like(a)), t) for a, t in zip(a16, t16)]
        outer = [_mm(t, i.astype(BF16)) for t, i in zip(t16, inner)]
        ts = [t - o for t, o in zip(ts, outer)]
        shift += 1
    return ts


def _solve_refined(a_strict, t_approx, rhs):
    t16 = [t.astype(BF16) for t in t_approx]
    x0 = [_mm(t, r.astype(BF16)) for t, r in zip(t16, rhs)]
    ax0 = _mm3_each(a_strict, x0)
    res = [r - x - ax for r, x, ax in zip(rhs, x0, ax0)]
    corr = [_mm(t, r.astype(BF16)) for t, r in zip(t16, res)]
    return [x + c for x, c in zip(x0, corr)]


def _causal_dwconv(head_ref, b, x, w, tail_ref):
    width = w.shape[0]
    n = x.shape[0]
    xp = jnp.concatenate([head_ref[b], x], axis=0)
    acc = pltpu.roll(xp, width - 1, axis=0)[SUBLANES:] * w[0:1]
    for i in range(1, width - 1):
        acc = acc + pltpu.roll(xp, width - 1 - i, axis=0)[SUBLANES:] * w[i:i + 1]
    acc = acc + x * w[width - 1:width]
    head_ref[b] = xp[n:]
    tail_ref[b] = xp[n + SUBLANES - (width - 1):]
    return acc


def _mixer_kernel(x_ref, cprev_ref, s0_ref, scprev_ref, gpre_ref, win_ref, cw_ref, alog_ref,
                  dtb_ref, gnw_ref, scw_ref, wo_ref, gpost_ref,
                  y_ref, cnew_ref, snew_ref, scnew_ref,
                  s_scr, chead, schead, q_scr, k_scr, v_scr, g_scr, beta_scr, o_scr,
                  gc_scr, u_scr, wq_scr, qkk_scr, zg_scr, sc_scr,
                  *, nb, tile, chunk):
    rows = nb * tile

    @pl.when(pl.program_id(1) == 0)
    def _():
        s_scr[...] = s0_ref[...]
        chead[...] = jnp.zeros(chead.shape, F32)
        schead[...] = jnp.zeros(schead.shape, F32)
        for b in range(nb):
            chead[b, SUBLANES - (GDN_CONV - 1):, :] = cprev_ref[b]
            schead[b, SUBLANES - (SC_CONV - 1):, :] = scprev_ref[b]

    x = x_ref[...].reshape(rows, D_MODEL)
    h = _rms(x, gpre_ref[...]).astype(BF16)

    def proj(lo, hi):
        return _mm(h, win_ref[:, lo:hi])

    qkv_in = proj(0, QKV_W)
    pba = proj(COL_BA, W_IN_COLS)
    z_in = proj(COL_Z, COL_B)
    sc_c = proj(COL_C, COL_H)
    sc_h = proj(COL_H, COL_BA)
    sc_b = proj(COL_B, COL_C)

    cw = cw_ref[...]
    for b in range(nb):
        r0 = b * tile
        acc = _causal_dwconv(chead, b, qkv_in[r0:r0 + tile], cw, cnew_ref)
        act = _silu(acc)
        for hh in range(HEADS):
            lo = hh * HEAD_DIM
            qh = act[:, lo:lo + HEAD_DIM]
            kh = act[:, GDN_W + lo:GDN_W + lo + HEAD_DIM]
            q_scr[r0:r0 + tile, lo:lo + HEAD_DIM] = (
                qh * lax.rsqrt(jnp.sum(qh * qh, axis=-1, keepdims=True) + EPS) * (HEAD_DIM ** -0.5))
            k_scr[r0:r0 + tile, lo:lo + HEAD_DIM] = (
                kh * lax.rsqrt(jnp.sum(kh * kh, axis=-1, keepdims=True) + EPS))
        v_scr[r0:r0 + tile, :] = act[:, 2 * GDN_W:]

    beta_scr[...] = jax.nn.sigmoid(pba)
    ga = pba + dtb_ref[...]
    softplus = jnp.maximum(ga, 0.0) + jnp.log1p(jnp.exp(-jnp.abs(ga)))
    g_scr[...] = -jnp.exp(alog_ref[...]) * softplus

    row = lax.broadcasted_iota(jnp.int32, (chunk, chunk), 0)
    col = lax.broadcasted_iota(jnp.int32, (chunk, chunk), 1)
    causal = row >= col
    strict = row > col
    tri = causal.astype(BF16)
    sel_r = lax.broadcasted_iota(jnp.int32, (SUBLANES, LANES), 0)
    sel_c = lax.broadcasted_iota(jnp.int32, (SUBLANES, LANES), 1)
    sel = (sel_c == sel_r + A_LANE0).astype(BF16)

    zg_scr[...] = _silu(z_in)
    cm = sc_c * sc_h
    scw = scw_ref[...]
    for b in range(nb):
        r0 = b * tile
        sc_scr[r0:r0 + tile, :] = _causal_dwconv(schead, b, cm[r0:r0 + tile], scw, scnew_ref)
    sc_scr[...] = sc_b * sc_scr[...]

    n_chunks = tile // chunk
    cpi = min(PREP_CHUNKS, n_chunks)
    units = [(b, k) for b in range(nb) for k in range(cpi)]
    prep_probs = [(ui, hh) for ui in range(len(units)) for hh in range(HEADS)]
    chain_probs = [(b, hh) for b in range(nb) for hh in range(HEADS)]

    def prep_step(it, carry):
        cidx = [it * cpi + k for _, k in units]
        r0s = [_aligned(b * tile + ci * chunk, chunk) for (b, _), ci in zip(units, cidx)]
        g3 = [_split3(g_scr[pl.ds(r0, chunk), :]) for r0 in r0s]
        gcs = [_mm(tri, t[0]) + (_mm(tri, t[1]) + _mm(tri, t[2])) for t in g3]
        gc3 = [_split3(gc) for gc in gcs]
        gcts = [_mm_nt(sel, t[0]) + (_mm_nt(sel, t[1]) + _mm_nt(sel, t[2])) for t in gc3]
        betas = [beta_scr[pl.ds(r0, chunk), :] for r0 in r0s]
        for r0, gc in zip(r0s, gcs):
            gc_scr[pl.ds(r0, chunk), :] = gc
        ops = []
        for ui, hh in prep_probs:
            lo = hh * HEAD_DIM
            gcol = gcs[ui][:, A_LANE0 + hh:A_LANE0 + hh + 1]
            grow = gcts[ui][hh:hh + 1, :]
            bcol = betas[ui][:, hh:hh + 1]
            qh = q_scr[pl.ds(r0s[ui], chunk), lo:lo + HEAD_DIM]
            kh = k_scr[pl.ds(r0s[ui], chunk), lo:lo + HEAD_DIM]
            vh = v_scr[pl.ds(r0s[ui], chunk), lo:lo + HEAD_DIM]
            kb = kh * bcol
            eg = jnp.exp(gcol)
            ops.append(dict(
                decay=jnp.exp(jnp.where(causal, gcol - grow, -jnp.inf)),
                q16=qh.astype(BF16), k16=kh.astype(BF16), kb16=kb.astype(BF16),
                rhs=jnp.concatenate([vh * bcol, kb * eg], axis=1),
                qg=qh * eg,
                kt=kh * jnp.exp(gcol[chunk - 1:chunk, :] - gcol)))
        a_mats = [jnp.where(strict, _mm_nt(o["kb16"], o["k16"]) * o["decay"], 0.0) for o in ops]
        qks = [_mm_nt(o["q16"], o["k16"]) * o["decay"] for o in ops]
        sols = _solve_refined(a_mats, _unit_lower_inverses(a_mats, row, col, chunk), [o["rhs"] for o in ops])
        for (ui, hh), o, qk, sol in zip(prep_probs, ops, qks, sols):
            lo = hh * HEAD_DIM
            r0 = r0s[ui]
            u_scr[pl.ds(r0, chunk), lo:lo + HEAD_DIM] = sol[:, :HEAD_DIM]
            wq_scr[pl.ds(2 * r0, 2 * chunk), lo:lo + HEAD_DIM] = jnp.concatenate(
                [sol[:, HEAD_DIM:], o["qg"]], axis=0).astype(BF16)
            qkk_scr[(units[ui][0] * n_chunks + cidx[ui]) * HEADS + hh] = jnp.concatenate(
                [qk, o["kt"].T], axis=0).astype(BF16)
        return carry

    def chain_step(c, carry):
        r0s = [_aligned(b * tile + c * chunk, chunk) for b in range(nb)]
        s_old = [s_scr[b, hh] for b, hh in chain_probs]
        ws = [_mm(wq_scr[pl.ds(2 * r0s[b], 2 * chunk), hh * HEAD_DIM:(hh + 1) * HEAD_DIM], s.astype(BF16))
              for (b, hh), s in zip(chain_probs, s_old)]
        v_new = [(u_scr[pl.ds(r0s[b], chunk), hh * HEAD_DIM:(hh + 1) * HEAD_DIM] - w[:chunk]).astype(BF16)
                 for (b, hh), w in zip(chain_probs, ws)]
        r2 = [_mm(qkk_scr[(b * n_chunks + c) * HEADS + hh], v) for (b, hh), v in zip(chain_probs, v_new)]
        for (b, hh), s, w, r in zip(chain_probs, s_old, ws, r2):
            lo = hh * HEAD_DIM
            o_scr[pl.ds(r0s[b], chunk), lo:lo + HEAD_DIM] = w[chunk:] + r[:chunk]
            glast = gc_scr[pl.ds(r0s[b] + (chunk - 1), 1), :][:, A_LANE0 + hh:A_LANE0 + hh + 1]
            s_scr[b, hh] = s * jnp.exp(glast) + r[chunk:]
        return carry

    _loop(n_chunks // cpi, prep_step)
    _loop(n_chunks, chain_step)
    snew_ref[...] = s_scr[...]

    gnw = gnw_ref[...]
    og = []
    for hh in range(HEADS):
        lo = hh * HEAD_DIM
        og.append(_rms(o_scr[:, lo:lo + HEAD_DIM], gnw) * zg_scr[:, lo:lo + HEAD_DIM])
    og = jnp.concatenate(og, axis=1).astype(BF16)
    mix = _mm(og, wo_ref[:GDN_W, :]) + _mm(sc_scr[...].astype(BF16), wo_ref[GDN_W:, :])
    y_ref[...] = (x_ref[...].reshape(rows, D_MODEL) + _rms(mix, gpost_ref[...])).reshape(nb, tile, D_MODEL)


def _ffn_kernel(x_ref, gpre_ref, wg_ref, wu_ref, wd_ref, gpost_ref, y_ref):
    x = x_ref[...]
    h = _rms(x, gpre_ref[...]).astype(BF16)
    f = None
    for lo in range(0, D_FF, FFN_COLS):
        act = (_silu(_mm(h, wg_ref[:, lo:lo + FFN_COLS])) * _mm(h, wu_ref[:, lo:lo + FFN_COLS])).astype(BF16)
        part = _mm(act, wd_ref[lo:lo + FFN_COLS, :])
        f = part if f is None else f + part
    y_ref[...] = x + _rms(f, gpost_ref[...])


def _const_spec(shape):
    return pl.BlockSpec(shape, lambda *_: (0,) * len(shape))


def _layer_spec(layer, shape):
    return pl.BlockSpec((None,) + shape, lambda *_: (layer, 0, 0))


def _mixer(x, conv_prev, s0, sc_prev, p, stacks, layer, *, nb, tile, chunk):
    batch, seq, _ = x.shape
    rows = nb * tile
    grid = (batch // nb, seq // tile)
    per_b = lambda i, j: (i, 0, 0)
    in_specs = [
        pl.BlockSpec((nb, tile, D_MODEL), lambda i, j: (i, j, 0)),
        pl.BlockSpec((nb, GDN_CONV - 1, QKV_W), per_b),
        pl.BlockSpec((nb, HEADS, HEAD_DIM, HEAD_DIM), lambda i, j: (i, 0, 0, 0)),
        pl.BlockSpec((nb, SC_CONV - 1, SC_W), per_b),
        _const_spec((1, D_MODEL)),
        _layer_spec(layer, (D_MODEL, W_IN_COLS)),
        _const_spec((GDN_CONV, QKV_W)),
        _const_spec((1, LANES)),
        _const_spec((1, LANES)),
        _const_spec((1, HEAD_DIM)),
        _const_spec((SC_CONV, SC_W)),
        _layer_spec(layer, (D_MODEL, D_MODEL)),
        _const_spec((1, D_MODEL)),
    ]
    out_specs = [
        pl.BlockSpec((nb, tile, D_MODEL), lambda i, j: (i, j, 0)),
        pl.BlockSpec((nb, GDN_CONV - 1, QKV_W), per_b),
        pl.BlockSpec((nb, HEADS, HEAD_DIM, HEAD_DIM), lambda i, j: (i, 0, 0, 0)),
        pl.BlockSpec((nb, SC_CONV - 1, SC_W), per_b),
    ]
    out_shape = [
        jax.ShapeDtypeStruct(x.shape, F32),
        jax.ShapeDtypeStruct(conv_prev.shape, F32),
        jax.ShapeDtypeStruct(s0.shape, F32),
        jax.ShapeDtypeStruct(sc_prev.shape, F32),
    ]
    scratch = [
        pltpu.VMEM((nb, HEADS, HEAD_DIM, HEAD_DIM), F32),
        pltpu.VMEM((nb, SUBLANES, QKV_W), F32),
        pltpu.VMEM((nb, SUBLANES, SC_W), F32),
        pltpu.VMEM((rows, GDN_W), F32),
        pltpu.VMEM((rows, GDN_W), F32),
        pltpu.VMEM((rows, GDN_W), F32),
        pltpu.VMEM((rows, LANES), F32),
        pltpu.VMEM((rows, LANES), F32),
        pltpu.VMEM((rows, GDN_W), F32),
        pltpu.VMEM((rows, LANES), F32),
        pltpu.VMEM((rows, GDN_W), F32),
        pltpu.VMEM((2 * rows, GDN_W), BF16),
        pltpu.VMEM((rows // chunk * HEADS, chunk + HEAD_DIM, chunk), BF16),
        pltpu.VMEM((rows, GDN_W), F32),
        pltpu.VMEM((rows, SC_W), F32),
    ]
    return pl.pallas_call(
        functools.partial(_mixer_kernel, nb=nb, tile=tile, chunk=chunk),
        grid=grid,
        in_specs=in_specs,
        out_specs=out_specs,
        out_shape=out_shape,
        scratch_shapes=scratch,
        compiler_params=pltpu.CompilerParams(
            dimension_semantics=("arbitrary", "arbitrary"),
            vmem_limit_bytes=VMEM_LIMIT_BYTES),
        name=f"mixer_t{tile}_c{chunk}",
    )(x, conv_prev, s0, sc_prev, p["norm_mix_pre"], stacks["w_in"], p["conv_qkv_w"], p["a_log"],
      p["dt_bias"], p["gdn_norm_w"], p["conv_sc_w"], stacks["w_o"], p["norm_mix_post"])


def _ffn(x2d, p, stacks, layer, *, tile):
    rows = x2d.shape[0]
    row_spec = pl.BlockSpec((tile, D_MODEL), lambda i: (i, 0))
    return pl.pallas_call(
        _ffn_kernel,
        grid=(rows // tile,),
        in_specs=[
            row_spec,
            _const_spec((1, D_MODEL)),
            _layer_spec(layer, (D_MODEL, D_FF)),
            _layer_spec(layer, (D_MODEL, D_FF)),
            _layer_spec(layer, (D_FF, D_MODEL)),
            _const_spec((1, D_MODEL)),
        ],
        out_specs=row_spec,
        out_shape=jax.ShapeDtypeStruct(x2d.shape, F32),
        compiler_params=pltpu.CompilerParams(
            dimension_semantics=("arbitrary",),
            vmem_limit_bytes=VMEM_LIMIT_BYTES),
        name=f"ffn_t{tile}",
    )(x2d, p["norm_ffn_pre"], stacks["w_gate"], stacks["w_up"], stacks["w_down"], p["norm_ffn_post"])


def _weight_stacks(w_in, w_o, w_gate, w_up, w_down):
    n_ba = 2 * HEADS
    w16 = w_in.astype(BF16)
    w_in_r = jnp.concatenate(
        [w16[..., :COL_B], w16[..., COL_B + n_ba:], w16[..., COL_B:COL_B + n_ba],
         jnp.zeros(w16.shape[:-1] + (LANES - n_ba,), BF16)], axis=-1)
    return {"w_in": w_in_r, "w_o": w_o.astype(BF16), "w_gate": w_gate.astype(BF16),
            "w_up": w_up.astype(BF16), "w_down": w_down.astype(BF16)}


def _layer_vectors(l, norm_mix_pre, conv_qkv_w, a_log, dt_bias, gdn_norm_w, conv_sc_w, norm_mix_post,
                   norm_ffn_pre, norm_ffn_post):
    lane_row = lambda v: jnp.zeros((1, LANES), F32).at[0, A_LANE0:A_LANE0 + HEADS].set(v.astype(F32))
    return {
        "norm_mix_pre": norm_mix_pre[l].reshape(1, D_MODEL),
        "conv_qkv_w": conv_qkv_w[l],
        "a_log": lane_row(a_log[l]),
        "dt_bias": lane_row(dt_bias[l]),
        "gdn_norm_w": gdn_norm_w[l].reshape(1, HEAD_DIM),
        "conv_sc_w": conv_sc_w[l],
        "norm_mix_post": norm_mix_post[l].reshape(1, D_MODEL),
        "norm_ffn_pre": norm_ffn_pre[l].reshape(1, D_MODEL),
        "norm_ffn_post": norm_ffn_post[l].reshape(1, D_MODEL),
    }


def _mixer_tiling(batch, seq):
    chunk = min(CHUNK, seq)
    if seq <= CHUNK:
        return batch, seq, chunk
    return batch, 256, chunk


def _run(x, conv0, s0, sc0, params, stacks):
    batch, seq, _ = x.shape
    nb, tile, chunk = _mixer_tiling(batch, seq)
    ffn_tile = min(FFN_TILE, batch * seq)
    convs, states, scs = [], [], []
    for l, p in enumerate(params):
        x, c, s, sc = _mixer(x, conv0[l], s0[l], sc0[l], p, stacks, l, nb=nb, tile=tile, chunk=chunk)
        x = _ffn(x.reshape(batch * seq, D_MODEL), p, stacks, l, tile=ffn_tile).reshape(batch, seq, D_MODEL)
        convs.append(c)
        states.append(s)
        scs.append(sc)
    return x, jnp.stack(convs), jnp.stack(states), jnp.stack(scs)


def kernel(x_prompt, x_sample, cache_gdn_conv, state_gdn, cache_sc_conv, norm_mix_pre, w_in, conv_qkv_w, a_log, dt_bias, gdn_norm_w, conv_sc_w, w_o, norm_mix_post, norm_ffn_pre, w_gate, w_up, w_down, norm_ffn_post):
    depth = w_in.shape[0]
    stacks = _weight_stacks(w_in, w_o, w_gate, w_up, w_down)
    params = [_layer_vectors(l, norm_mix_pre, conv_qkv_w, a_log, dt_bias, gdn_norm_w, conv_sc_w,
                             norm_mix_post, norm_ffn_pre, norm_ffn_post) for l in range(depth)]
    bp = x_prompt.shape[0]
    zc = jnp.zeros((depth, bp, GDN_CONV - 1, QKV_W), F32)
    zs = jnp.zeros((depth, bp, HEADS, HEAD_DIM, HEAD_DIM), F32)
    zsc = jnp.zeros((depth, bp, SC_CONV - 1, SC_W), F32)
    y_prompt, conv_p, state_p, sc_p = _run(x_prompt, zc, zs, zsc, params, stacks)
    y_sample, conv_s, state_s, sc_s = _run(x_sample, cache_gdn_conv, state_gdn, cache_sc_conv, params, stacks)
    return (y_prompt, y_sample, conv_p, state_p, sc_p, conv_s, state_s, sc_s)
```

```python
import functools

import jax
import jax.numpy as jnp
from jax import lax
from jax.experimental import pallas as pl
from jax.experimental.pallas import tpu as pltpu

F32 = jnp.float32
BF16 = jnp.bfloat16

D_MODEL = 1024
HEADS = 4
HEAD_DIM = 128
GDN_W = HEADS * HEAD_DIM
SC_W = D_MODEL - GDN_W
QKV_W = 3 * GDN_W
GDN_CONV = 4
SC_CONV = 3
D_FF = 2816
CHUNK = 64
EPS = 1e-6

QKVZ_W = QKV_W + GDN_W
LANES = 128
SUBLANES = 8
A_LANE0 = HEADS

VMEM_LIMIT_BYTES = 56 * 1024 * 1024
MXU_TILE = 256
FFN_SLABS = (3 * MXU_TILE,) * 3 + (2 * MXU_TILE,)
assert sum(FFN_SLABS) == D_FF
FFN_TILE = 1024
PREP_CHUNKS = 4
MAX_UNROLLED_TRIPS = 4


def _rms(x, g):
    return x * lax.rsqrt(jnp.mean(x * x, axis=-1, keepdims=True) + EPS) * g


def _silu(x):
    return x * jax.nn.sigmoid(x)


def _mm(a, b):
    return jnp.dot(a, b, preferred_element_type=F32)


def _mm_nt(a, b):
    return lax.dot_general(a, b, (((1,), (1,)), ((), ())), preferred_element_type=F32)


def _aligned(index, multiple):
    return index if isinstance(index, int) else pl.multiple_of(index, multiple)


def _loop(trips, body):
    if trips <= MAX_UNROLLED_TRIPS:
        for i in range(trips):
            body(i, 0)
    else:
        lax.fori_loop(0, trips, body, 0)


def _split2(x):
    hi = x.astype(BF16)
    lo = (x - hi.astype(F32)).astype(BF16)
    return hi, lo


def _split3(x):
    h1 = x.astype(BF16)
    r1 = x - h1.astype(F32)
    h2 = r1.astype(BF16)
    h3 = (r1 - h2.astype(F32)).astype(BF16)
    return h1, h2, h3


def _mm3_each(lhs, rhs):
    ls = [_split2(a) for a in lhs]
    rs = [_split2(b) for b in rhs]
    hh = [_mm(l[0], r[0]) for l, r in zip(ls, rs)]
    hl = [_mm(l[0], r[1]) for l, r in zip(ls, rs)]
    lh = [_mm(l[1], r[0]) for l, r in zip(ls, rs)]
    return [a + (b + c) for a, b, c in zip(hh, hl, lh)]


def _unit_lower_inverses(a_strict, row, col, size):
    eye = (row == col).astype(F32)
    ts = [eye - jnp.where((row >> 1) == (col >> 1), a, 0.0) for a in a_strict]
    a16 = [a.astype(BF16) for a in a_strict]
    shift = 1
    while (1 << shift) < size:
        off = ((row >> (shift + 1)) == (col >> (shift + 1))) & ((row >> shift) != (col >> shift))
        t16 = [t.astype(BF16) for t in ts]
        inner = [_mm(jnp.where(off, a, jnp.zeros_like(a)), t) for a, t in zip(a16, t16)]
        outer = [_mm(t, i.astype(BF16)) for t, i in zip(t16, inner)]
        ts = [t - o for t, o in zip(ts, outer)]
        shift += 1
    return ts


def _solve_refined(a_strict, t_approx, rhs):
    t16 = [t.astype(BF16) for t in t_approx]
    x0 = [_mm(t, r.astype(BF16)) for t, r in zip(t16, rhs)]
    ax0 = _mm3_each(a_strict, x0)
    res = [r - x - ax for r, x, ax in zip(rhs, x0, ax0)]
    corr = [_mm(t, r.astype(BF16)) for t, r in zip(t16, res)]
    return [x + c for x, c in zip(x0, corr)]


def _causal_dwconv(head_ref, b, x, w, tail_ref):
    width = w.shape[0]
    n, channels = x.shape
    xp = jnp.concatenate([head_ref[b], x], axis=0)
    groups = xp.reshape(n // SUBLANES + 1, SUBLANES, channels)
    sublane = lax.broadcasted_iota(jnp.int32, (1, SUBLANES, channels), 1)

    def delayed(s):
        r = pltpu.roll(groups, s, axis=1)
        return jnp.where(sublane < s, r[:-1], r[1:]).reshape(n, channels)

    acc = delayed(width - 1) * w[0:1]
    for i in range(1, width - 1):
        acc = acc + delayed(width - 1 - i) * w[i:i + 1]
    acc = acc + x * w[width - 1:width]
    head_ref[b] = xp[n:]
    tail_ref[b] = xp[n + SUBLANES - (width - 1):]
    return acc


def _mixer_kernel(x_ref, cprev_ref, s0_ref, scprev_ref, gpre_ref, wqkvz_ref, wsc_ref, wba_ref, cw_ref, alog_ref,
                  dtb_ref, gnw_ref, scw_ref, wo_ref, gpost_ref,
                  y_ref, cnew_ref, snew_ref, scnew_ref,
                  s_scr, chead, schead, q_scr, k_scr, v_scr, g_scr, beta_scr, o_scr,
                  gc_scr, u_scr, wq_scr, qkk_scr, zg_scr, sc_scr,
                  *, nb, tile, chunk):
    rows = nb * tile

    @pl.when(pl.program_id(1) == 0)
    def _():
        s_scr[...] = s0_ref[...]
        chead[...] = jnp.zeros(chead.shape, F32)
        schead[...] = jnp.zeros(schead.shape, F32)
        for b in range(nb):
            chead[b, SUBLANES - (GDN_CONV - 1):, :] = cprev_ref[b]
            schead[b, SUBLANES - (SC_CONV - 1):, :] = scprev_ref[b]

    x = x_ref[...].reshape(rows, D_MODEL)
    h = _rms(x, gpre_ref[...]).astype(BF16)

    qkv_in = _mm(h, wqkvz_ref[:, :QKV_W])
    pba = _mm(h, wba_ref[...])
    z_in = _mm(h, wqkvz_ref[:, QKV_W:])
    sc_c = _mm(h, wsc_ref[:, SC_W:2 * SC_W])
    sc_h = _mm(h, wsc_ref[:, 2 * SC_W:])
    sc_b = _mm(h, wsc_ref[:, :SC_W])

    cw = cw_ref[...]
    for b in range(nb):
        r0 = b * tile
        acc = _causal_dwconv(chead, b, qkv_in[r0:r0 + tile], cw, cnew_ref)
        act = _silu(acc)
        for hh in range(HEADS):
            lo = hh * HEAD_DIM
            qh = act[:, lo:lo + HEAD_DIM]
            kh = act[:, GDN_W + lo:GDN_W + lo + HEAD_DIM]
            q_scr[r0:r0 + tile, lo:lo + HEAD_DIM] = (
                qh * lax.rsqrt(jnp.sum(qh * qh, axis=-1, keepdims=True) + EPS) * (HEAD_DIM ** -0.5))
            k_scr[r0:r0 + tile, lo:lo + HEAD_DIM] = (
                kh * lax.rsqrt(jnp.sum(kh * kh, axis=-1, keepdims=True) + EPS))
        v_scr[r0:r0 + tile, :] = act[:, 2 * GDN_W:]

    beta_scr[...] = jax.nn.sigmoid(pba)
    ga = pba + dtb_ref[...]
    softplus = jnp.maximum(ga, 0.0) + jnp.log1p(jnp.exp(-jnp.abs(ga)))
    g_scr[...] = -jnp.exp(alog_ref[...]) * softplus

    row = lax.broadcasted_iota(jnp.int32, (chunk, chunk), 0)
    col = lax.broadcasted_iota(jnp.int32, (chunk, chunk), 1)
    causal = row >= col
    strict = row > col
    tri = causal.astype(BF16)
    sel_r = lax.broadcasted_iota(jnp.int32, (SUBLANES, LANES), 0)
    sel_c = lax.broadcasted_iota(jnp.int32, (SUBLANES, LANES), 1)
    sel = (sel_c == sel_r + A_LANE0).astype(BF16)

    zg_scr[...] = _silu(z_in)
    cm = sc_c * sc_h
    scw = scw_ref[...]
    for b in range(nb):
        r0 = b * tile
        sc_scr[r0:r0 + tile, :] = _causal_dwconv(schead, b, cm[r0:r0 + tile], scw, scnew_ref)
    sc_scr[...] = sc_b * sc_scr[...]

    n_chunks = tile // chunk
    cpi = min(PREP_CHUNKS, n_chunks)
    units = [(b, k) for b in range(nb) for k in range(cpi)]
    prep_probs = [(ui, hh) for ui in range(len(units)) for hh in range(HEADS)]
    chain_probs = [(b, hh) for b in range(nb) for hh in range(HEADS)]

    def prep_step(it, carry):
        cidx = [it * cpi + k for _, k in units]
        r0s = [_aligned(b * tile + ci * chunk, chunk) for (b, _), ci in zip(units, cidx)]
        g3 = [_split3(g_scr[pl.ds(r0, chunk), :]) for r0 in r0s]
        gcs = [_mm(tri, t[0]) + (_mm(tri, t[1]) + _mm(tri, t[2])) for t in g3]
        gc3 = [_split3(gc) for gc in gcs]
        gcts = [_mm_nt(sel, t[0]) + (_mm_nt(sel, t[1]) + _mm_nt(sel, t[2])) for t in gc3]
        betas = [beta_scr[pl.ds(r0, chunk), :] for r0 in r0s]
        for r0, gc in zip(r0s, gcs):
            gc_scr[pl.ds(r0, chunk), :] = gc
        ops = []
        for ui, hh in prep_probs:
            lo = hh * HEAD_DIM
            gcol = gcs[ui][:, A_LANE0 + hh:A_LANE0 + hh + 1]
            grow = gcts[ui][hh:hh + 1, :]
            bcol = betas[ui][:, hh:hh + 1]
            qh = q_scr[pl.ds(r0s[ui], chunk), lo:lo + HEAD_DIM]
            kh = k_scr[pl.ds(r0s[ui], chunk), lo:lo + HEAD_DIM]
            vh = v_scr[pl.ds(r0s[ui], chunk), lo:lo + HEAD_DIM]
            kb = kh * bcol
            eg = jnp.exp(gcol)
            ops.append(dict(
                decay=jnp.exp(jnp.where(causal, gcol - grow, -jnp.inf)),
                q16=qh.astype(BF16), k16=kh.astype(BF16), kb16=kb.astype(BF16),
                rhs=jnp.concatenate([vh * bcol, kb * eg], axis=1),
                qg=qh * eg,
                kt=kh * jnp.exp(gcol[chunk - 1:chunk, :] - gcol)))
        a_mats = [jnp.where(strict, _mm_nt(o["kb16"], o["k16"]) * o["decay"], 0.0) for o in ops]
        qks = [_mm_nt(o["q16"], o["k16"]) * o["decay"] for o in ops]
        sols = _solve_refined(a_mats, _unit_lower_inverses(a_mats, row, col, chunk), [o["rhs"] for o in ops])
        for (ui, hh), o, qk, sol in zip(prep_probs, ops, qks, sols):
            lo = hh * HEAD_DIM
            r0 = r0s[ui]
            u_scr[pl.ds(r0, chunk), lo:lo + HEAD_DIM] = sol[:, :HEAD_DIM]
            wq_scr[pl.ds(2 * r0, 2 * chunk), lo:lo + HEAD_DIM] = jnp.concatenate(
                [sol[:, HEAD_DIM:], o["qg"]], axis=0).astype(BF16)
            qkk_scr[(units[ui][0] * n_chunks + cidx[ui]) * HEADS + hh] = jnp.concatenate(
                [qk, o["kt"].T], axis=0).astype(BF16)
        return carry

    def chain_step(c, carry):
        r0s = [_aligned(b * tile + c * chunk, chunk) for b in range(nb)]
        s_old = [s_scr[b, hh] for b, hh in chain_probs]
        ws = [_mm(wq_scr[pl.ds(2 * r0s[b], 2 * chunk), hh * HEAD_DIM:(hh + 1) * HEAD_DIM], s.astype(BF16))
              for (b, hh), s in zip(chain_probs, s_old)]
        v_new = [(u_scr[pl.ds(r0s[b], chunk), hh * HEAD_DIM:(hh + 1) * HEAD_DIM] - w[:chunk]).astype(BF16)
                 for (b, hh), w in zip(chain_probs, ws)]
        r2 = [_mm(qkk_scr[(b * n_chunks + c) * HEADS + hh], v) for (b, hh), v in zip(chain_probs, v_new)]
        for (b, hh), s, w, r in zip(chain_probs, s_old, ws, r2):
            lo = hh * HEAD_DIM
            o_scr[pl.ds(r0s[b], chunk), lo:lo + HEAD_DIM] = w[chunk:] + r[:chunk]
            glast = gc_scr[pl.ds(r0s[b] + (chunk - 1), 1), :][:, A_LANE0 + hh:A_LANE0 + hh + 1]
            s_scr[b, hh] = s * jnp.exp(glast) + r[chunk:]
        return carry

    _loop(n_chunks // cpi, prep_step)
    _loop(n_chunks, chain_step)
    snew_ref[...] = s_scr[...]

    gnw = gnw_ref[...]
    og = []
    for hh in range(HEADS):
        lo = hh * HEAD_DIM
        og.append(_rms(o_scr[:, lo:lo + HEAD_DIM], gnw) * zg_scr[:, lo:lo + HEAD_DIM])
    og = jnp.concatenate(og, axis=1).astype(BF16)
    mix = _mm(og, wo_ref[:GDN_W, :]) + _mm(sc_scr[...].astype(BF16), wo_ref[GDN_W:, :])
    y_ref[...] = (x_ref[...].reshape(rows, D_MODEL) + _rms(mix, gpost_ref[...])).reshape(nb, tile, D_MODEL)


def _ffn_kernel(x_ref, gpre_ref, wg_ref, wu_ref, wd_ref, gpost_ref, y_ref):
    x = x_ref[...]
    h = _rms(x, gpre_ref[...]).astype(BF16)
    f = None
    lo = 0
    for width in FFN_SLABS:
        act = (_silu(_mm(h, wg_ref[:, lo:lo + width])) * _mm(h, wu_ref[:, lo:lo + width])).astype(BF16)
        part = _mm(act, wd_ref[lo:lo + width, :])
        f = part if f is None else f + part
        lo += width
    y_ref[...] = x + _rms(f, gpost_ref[...])


def _const_spec(shape):
    return pl.BlockSpec(shape, lambda *_: (0,) * len(shape))


def _layer_spec(layer, shape):
    return pl.BlockSpec((None,) + shape, lambda *_: (layer, 0, 0))


def _mixer(x, conv_prev, s0, sc_prev, p, stacks, layer, *, nb, tile, chunk):
    batch, seq, _ = x.shape
    rows = nb * tile
    grid = (batch // nb, seq // tile)
    per_b = lambda i, j: (i, 0, 0)
    in_specs = [
        pl.BlockSpec((nb, tile, D_MODEL), lambda i, j: (i, j, 0)),
        pl.BlockSpec((nb, GDN_CONV - 1, QKV_W), per_b),
        pl.BlockSpec((nb, HEADS, HEAD_DIM, HEAD_DIM), lambda i, j: (i, 0, 0, 0)),
        pl.BlockSpec((nb, SC_CONV - 1, SC_W), per_b),
        _const_spec((1, D_MODEL)),
        _layer_spec(layer, (D_MODEL, QKVZ_W)),
        _layer_spec(layer, (D_MODEL, 3 * SC_W)),
        _layer_spec(layer, (D_MODEL, LANES)),
        _const_spec((GDN_CONV, QKV_W)),
        _const_spec((1, LANES)),
        _const_spec((1, LANES)),
        _const_spec((1, HEAD_DIM)),
        _const_spec((SC_CONV, SC_W)),
        _layer_spec(layer, (D_MODEL, D_MODEL)),
        _const_spec((1, D_MODEL)),
    ]
    out_specs = [
        pl.BlockSpec((nb, tile, D_MODEL), lambda i, j: (i, j, 0)),
        pl.BlockSpec((nb, GDN_CONV - 1, QKV_W), per_b),
        pl.BlockSpec((nb, HEADS, HEAD_DIM, HEAD_DIM), lambda i, j: (i, 0, 0, 0)),
        pl.BlockSpec((nb, SC_CONV - 1, SC_W), per_b),
    ]
    out_shape = [
        jax.ShapeDtypeStruct(x.shape, F32),
        jax.ShapeDtypeStruct(conv_prev.shape, F32),
        jax.ShapeDtypeStruct(s0.shape, F32),
        jax.ShapeDtypeStruct(sc_prev.shape, F32),
    ]
    scratch = [
        pltpu.VMEM((nb, HEADS, HEAD_DIM, HEAD_DIM), F32),
        pltpu.VMEM((nb, SUBLANES, QKV_W), F32),
        pltpu.VMEM((nb, SUBLANES, SC_W), F32),
        pltpu.VMEM((rows, GDN_W), F32),
        pltpu.VMEM((rows, GDN_W), F32),
        pltpu.VMEM((rows, GDN_W), F32),
        pltpu.VMEM((rows, LANES), F32),
        pltpu.VMEM((rows, LANES), F32),
        pltpu.VMEM((rows, GDN_W), F32),
        pltpu.VMEM((rows, LANES), F32),
        pltpu.VMEM((rows, GDN_W), F32),
        pltpu.VMEM((2 * rows, GDN_W), BF16),
        pltpu.VMEM((rows // chunk * HEADS, chunk + HEAD_DIM, chunk), BF16),
        pltpu.VMEM((rows, GDN_W), F32),
        pltpu.VMEM((rows, SC_W), F32),
    ]
    return pl.pallas_call(
        functools.partial(_mixer_kernel, nb=nb, tile=tile, chunk=chunk),
        grid=grid,
        in_specs=in_specs,
        out_specs=out_specs,
        out_shape=out_shape,
        scratch_shapes=scratch,
        compiler_params=pltpu.CompilerParams(
            dimension_semantics=("arbitrary", "arbitrary"),
            vmem_limit_bytes=VMEM_LIMIT_BYTES),
        name=f"mixer_t{tile}_c{chunk}",
    )(x, conv_prev, s0, sc_prev, p["norm_mix_pre"], stacks["w_in"], stacks["w_sc"], stacks["w_ba"],
      p["conv_qkv_w"], p["a_log"],
      p["dt_bias"], p["gdn_norm_w"], p["conv_sc_w"], stacks["w_o"], p["norm_mix_post"])


def _ffn(x2d, p, stacks, layer, *, tile):
    rows = x2d.shape[0]
    row_spec = pl.BlockSpec((tile, D_MODEL), lambda i: (i, 0))
    return pl.pallas_call(
        _ffn_kernel,
        grid=(rows // tile,),
        in_specs=[
            row_spec,
            _const_spec((1, D_MODEL)),
            _layer_spec(layer, (D_MODEL, D_FF)),
            _layer_spec(layer, (D_MODEL, D_FF)),
            _layer_spec(layer, (D_FF, D_MODEL)),
            _const_spec((1, D_MODEL)),
        ],
        out_specs=row_spec,
        out_shape=jax.ShapeDtypeStruct(x2d.shape, F32),
        compiler_params=pltpu.CompilerParams(
            dimension_semantics=("arbitrary",),
            vmem_limit_bytes=VMEM_LIMIT_BYTES),
        name=f"ffn_t{tile}",
    )(x2d, p["norm_ffn_pre"], stacks["w_gate"], stacks["w_up"], stacks["w_down"], p["norm_ffn_post"])


def _weight_stacks(w_in, w_o, w_gate, w_up, w_down):
    n_ba = 2 * HEADS
    w16 = w_in.astype(BF16)
    w_ba = jnp.pad(w16[..., QKVZ_W:QKVZ_W + n_ba], ((0, 0), (0, 0), (0, LANES - n_ba)))
    return {"w_in": w16, "w_sc": w16[..., QKVZ_W + n_ba:], "w_ba": w_ba, "w_o": w_o.astype(BF16),
            "w_gate": w_gate.astype(BF16), "w_up": w_up.astype(BF16), "w_down": w_down.astype(BF16)}


def _layer_vectors(l, norm_mix_pre, conv_qkv_w, a_log, dt_bias, gdn_norm_w, conv_sc_w, norm_mix_post,
                   norm_ffn_pre, norm_ffn_post):
    lane_row = lambda v: jnp.zeros((1, LANES), F32).at[0, A_LANE0:A_LANE0 + HEADS].set(v.astype(F32))
    return {
        "norm_mix_pre": norm_mix_pre[l].reshape(1, D_MODEL),
        "conv_qkv_w": conv_qkv_w[l],
        "a_log": lane_row(a_log[l]),
        "dt_bias": lane_row(dt_bias[l]),
        "gdn_norm_w": gdn_norm_w[l].reshape(1, HEAD_DIM),
        "conv_sc_w": conv_sc_w[l],
        "norm_mix_post": norm_mix_post[l].reshape(1, D_MODEL),
        "norm_ffn_pre": norm_ffn_pre[l].reshape(1, D_MODEL),
        "norm_ffn_post": norm_ffn_post[l].reshape(1, D_MODEL),
    }


def _mixer_tiling(batch, seq):
    chunk = min(CHUNK, seq)
    if seq <= CHUNK:
        return batch, seq, chunk
    return batch, 256, chunk


def _run(x, conv0, s0, sc0, params, stacks):
    batch, seq, _ = x.shape
    nb, tile, chunk = _mixer_tiling(batch, seq)
    ffn_tile = min(FFN_TILE, batch * seq)
    convs, states, scs = [], [], []
    for l, p in enumerate(params):
        x, c, s, sc = _mixer(x, conv0[l], s0[l], sc0[l], p, stacks, l, nb=nb, tile=tile, chunk=chunk)
        x = _ffn(x.reshape(batch * seq, D_MODEL), p, stacks, l, tile=ffn_tile).reshape(batch, seq, D_MODEL)
        convs.append(c)
        states.append(s)
        scs.append(sc)
    return x, jnp.stack(convs), jnp.stack(states), jnp.stack(scs)


def kernel(x_prompt, x_sample, cache_gdn_conv, state_gdn, cache_sc_conv, norm_mix_pre, w_in, conv_qkv_w, a_log, dt_bias, gdn_norm_w, conv_sc_w, w_o, norm_mix_post, norm_ffn_pre, w_gate, w_up, w_down, norm_ffn_post):
    depth = w_in.shape[0]
    stacks = _weight_stacks(w_in, w_o, w_gate, w_up, w_down)
    params = [_layer_vectors(l, norm_mix_pre, conv_qkv_w, a_log, dt_bias, gdn_norm_w, conv_sc_w,
                             norm_mix_post, norm_ffn_pre, norm_ffn_post) for l in range(depth)]
    bp = x_prompt.shape[0]
    zc = jnp.zeros((depth, bp, GDN_CONV - 1, QKV_W), F32)
    zs = jnp.zeros((depth, bp, HEADS, HEAD_DIM, HEAD_DIM), F32)
    zsc = jnp.zeros((depth, bp, SC_CONV - 1, SC_W), F32)
    y_prompt, conv_p, state_p, sc_p = _run(x_prompt, zc, zs, zsc, params, stacks)
    y_sample, conv_s, state_s, sc_s = _run(x_sample, cache_gdn_conv, state_gdn, cache_sc_conv, params, stacks)
    return (y_prompt, y_sample, conv_p, state_p, sc_p, conv_s, state_s, sc_s)
```

```python
import functools

import jax
import jax.numpy as jnp
from jax import lax
from jax.experimental import pallas as pl
from jax.experimental.pallas import tpu as pltpu

F32 = jnp.float32
BF16 = jnp.bfloat16

D_MODEL = 1024
HEADS = 4
HEAD_DIM = 128
GDN_W = HEADS * HEAD_DIM
SC_W = D_MODEL - GDN_W
QKV_W = 3 * GDN_W
GDN_CONV = 4
SC_CONV = 3
D_FF = 2816
CHUNK = 64
EPS = 1e-6

QKVZ_W = QKV_W + GDN_W
LANES = 128
SUBLANES = 8
A_LANE0 = HEADS

VMEM_LIMIT_BYTES = 56 * 1024 * 1024
MXU_TILE = 256
FFN_SLABS = (3 * MXU_TILE,) * 3 + (2 * MXU_TILE,)
assert sum(FFN_SLABS) == D_FF
FFN_TILE = 1024
PREP_CHUNKS = 4
MAX_UNROLLED_TRIPS = 4


def _rms(x, g):
    return x * lax.rsqrt(jnp.mean(x * x, axis=-1, keepdims=True) + EPS) * g


def _silu(x):
    return x * jax.nn.sigmoid(x)


def _mm(a, b):
    return jnp.dot(a, b, preferred_element_type=F32)


def _mm_nt(a, b):
    return lax.dot_general(a, b, (((1,), (1,)), ((), ())), preferred_element_type=F32)


def _aligned(index, multiple):
    return index if isinstance(index, int) else pl.multiple_of(index, multiple)


def _loop(trips, body):
    if trips <= MAX_UNROLLED_TRIPS:
        for i in range(trips):
            body(i, 0)
    else:
        lax.fori_loop(0, trips, body, 0)


def _split2(x):
    hi = x.astype(BF16)
    lo = (x - hi.astype(F32)).astype(BF16)
    return hi, lo


def _split3(x):
    h1 = x.astype(BF16)
    r1 = x - h1.astype(F32)
    h2 = r1.astype(BF16)
    h3 = (r1 - h2.astype(F32)).astype(BF16)
    return h1, h2, h3


def _unit_lower_inverses(a_strict, row, col, size):
    eye = (row == col).astype(F32)
    ts = [eye - jnp.where((row >> 1) == (col >> 1), a, 0.0) for a in a_strict]
    a16 = [a.astype(BF16) for a in a_strict]
    shift = 1
    while (1 << shift) < size:
        off = ((row >> (shift + 1)) == (col >> (shift + 1))) & ((row >> shift) != (col >> shift))
        t16 = [t.astype(BF16) for t in ts]
        inner = [_mm(jnp.where(off, a, jnp.zeros_like(a)), t) for a, t in zip(a16, t16)]
        outer = [_mm(t, i.astype(BF16)) for t, i in zip(t16, inner)]
        ts = [t - o for t, o in zip(ts, outer)]
        shift += 1
    return ts


def _solve_refined(a_strict, t_approx, rhs):
    t16 = [t.astype(BF16) for t in t_approx]
    x0 = [_mm(t, r.astype(BF16)).astype(BF16) for t, r in zip(t16, rhs)]
    a_parts = [_split2(a) for a in a_strict]
    ax_hi = [_mm(a[0], x) for a, x in zip(a_parts, x0)]
    ax_lo = [_mm(a[1], x) for a, x in zip(a_parts, x0)]
    res = [r - x.astype(F32) - (h + l) for r, x, h, l in zip(rhs, x0, ax_hi, ax_lo)]
    corr = [_mm(t, r.astype(BF16)) for t, r in zip(t16, res)]
    return [x.astype(F32) + c for x, c in zip(x0, corr)]


def _causal_dwconv(head_ref, b, x, w, tail_ref):
    width = w.shape[0]
    n = x.shape[0]
    xp = jnp.concatenate([head_ref[b], x], axis=0)
    acc = pltpu.roll(xp, width - 1, axis=0)[SUBLANES:] * w[0:1]
    for i in range(1, width - 1):
        acc = acc + pltpu.roll(xp, width - 1 - i, axis=0)[SUBLANES:] * w[i:i + 1]
    acc = acc + x * w[width - 1:width]
    head_ref[b] = xp[n:]
    tail_ref[b] = xp[n + SUBLANES - (width - 1):]
    return acc


def _mixer_kernel(x_ref, cprev_ref, s0_ref, scprev_ref, gpre_ref, wqkvz_ref, wsc_ref, wba_ref, cw_ref, alog_ref,
                  dtb_ref, gnw_ref, scw_ref, wo_ref, gpost_ref,
                  y_ref, cnew_ref, snew_ref, scnew_ref,
                  s_scr, chead, schead, q_scr, k_scr, v_scr, g_scr, beta_scr, o_scr,
                  gc_scr, u_scr, wq_scr, qkk_scr, zg_scr, sc_scr,
                  *, nb, tile, chunk):
    rows = nb * tile

    @pl.when(pl.program_id(1) == 0)
    def _():
        s_scr[...] = s0_ref[...]
        chead[...] = jnp.zeros(chead.shape, F32)
        schead[...] = jnp.zeros(schead.shape, F32)
        for b in range(nb):
            chead[b, SUBLANES - (GDN_CONV - 1):, :] = cprev_ref[b]
            schead[b, SUBLANES - (SC_CONV - 1):, :] = scprev_ref[b]

    x = x_ref[...].reshape(rows, D_MODEL)
    h = _rms(x, gpre_ref[...]).astype(BF16)

    qkv_in = _mm(h, wqkvz_ref[:, :QKV_W])
    pba = _mm(h, wba_ref[...])
    z_in = _mm(h, wqkvz_ref[:, QKV_W:])
    sc_c = _mm(h, wsc_ref[:, SC_W:2 * SC_W])
    sc_h = _mm(h, wsc_ref[:, 2 * SC_W:])
    sc_b = _mm(h, wsc_ref[:, :SC_W])

    cw = cw_ref[...]
    for b in range(nb):
        r0 = b * tile
        acc = _causal_dwconv(chead, b, qkv_in[r0:r0 + tile], cw, cnew_ref)
        act = _silu(acc)
        for hh in range(HEADS):
            lo = hh * HEAD_DIM
            qh = act[:, lo:lo + HEAD_DIM]
            kh = act[:, GDN_W + lo:GDN_W + lo + HEAD_DIM]
            q_scr[r0:r0 + tile, lo:lo + HEAD_DIM] = (
                qh * lax.rsqrt(jnp.sum(qh * qh, axis=-1, keepdims=True) + EPS) * (HEAD_DIM ** -0.5))
            k_scr[r0:r0 + tile, lo:lo + HEAD_DIM] = (
                kh * lax.rsqrt(jnp.sum(kh * kh, axis=-1, keepdims=True) + EPS))
        v_scr[r0:r0 + tile, :] = act[:, 2 * GDN_W:]

    beta_scr[...] = jax.nn.sigmoid(pba)
    ga = pba + dtb_ref[...]
    softplus = jnp.maximum(ga, 0.0) + jnp.log1p(jnp.exp(-jnp.abs(ga)))
    g_scr[...] = -jnp.exp(alog_ref[...]) * softplus

    row = lax.broadcasted_iota(jnp.int32, (chunk, chunk), 0)
    col = lax.broadcasted_iota(jnp.int32, (chunk, chunk), 1)
    causal = row >= col
    strict = row > col
    tri = causal.astype(BF16)
    sel_r = lax.broadcasted_iota(jnp.int32, (SUBLANES, LANES), 0)
    sel_c = lax.broadcasted_iota(jnp.int32, (SUBLANES, LANES), 1)
    sel = (sel_c == sel_r + A_LANE0).astype(BF16)

    zg_scr[...] = _silu(z_in)
    cm = sc_c * sc_h
    scw = scw_ref[...]
    for b in range(nb):
        r0 = b * tile
        sc_scr[r0:r0 + tile, :] = _causal_dwconv(schead, b, cm[r0:r0 + tile], scw, scnew_ref)
    sc_scr[...] = sc_b * sc_scr[...]

    n_chunks = tile // chunk
    cpi = min(PREP_CHUNKS, n_chunks)
    units = [(b, k) for b in range(nb) for k in range(cpi)]
    prep_probs = [(ui, hh) for ui in range(len(units)) for hh in range(HEADS)]
    chain_probs = [(b, hh) for b in range(nb) for hh in range(HEADS)]

    def prep_step(it, carry):
        cidx = [it * cpi + k for _, k in units]
        r0s = [_aligned(b * tile + ci * chunk, chunk) for (b, _), ci in zip(units, cidx)]
        g3 = [_split3(g_scr[pl.ds(r0, chunk), :]) for r0 in r0s]
        gcs = [_mm(tri, t[0]) + (_mm(tri, t[1]) + _mm(tri, t[2])) for t in g3]
        gc3 = [_split3(gc) for gc in gcs]
        gcts = [_mm_nt(sel, t[0]) + (_mm_nt(sel, t[1]) + _mm_nt(sel, t[2])) for t in gc3]
        betas = [beta_scr[pl.ds(r0, chunk), :] for r0 in r0s]
        for r0, gc in zip(r0s, gcs):
            gc_scr[pl.ds(r0, chunk), :] = gc
        ops = []
        for ui, hh in prep_probs:
            lo = hh * HEAD_DIM
            gcol = gcs[ui][:, A_LANE0 + hh:A_LANE0 + hh + 1]
            grow = gcts[ui][hh:hh + 1, :]
            bcol = betas[ui][:, hh:hh + 1]
            qh = q_scr[pl.ds(r0s[ui], chunk), lo:lo + HEAD_DIM]
            kh = k_scr[pl.ds(r0s[ui], chunk), lo:lo + HEAD_DIM]
            vh = v_scr[pl.ds(r0s[ui], chunk), lo:lo + HEAD_DIM]
            kb = kh * bcol
            eg = jnp.exp(gcol)
            ops.append(dict(
                decay=jnp.exp(jnp.where(causal, gcol - grow, -jnp.inf)),
                q16=qh.astype(BF16), k16=kh.astype(BF16), kb16=kb.astype(BF16),
                rhs=jnp.concatenate([vh * bcol, kb * eg], axis=1),
                qg=qh * eg,
                kt=kh * jnp.exp(gcol[chunk - 1:chunk, :] - gcol)))
        kq = [_mm_nt(jnp.concatenate([o["kb16"], o["q16"]], axis=0), o["k16"]) for o in ops]
        a_mats = [jnp.where(strict, p[:chunk] * o["decay"], 0.0) for p, o in zip(kq, ops)]
        qks = [p[chunk:] * o["decay"] for p, o in zip(kq, ops)]
        sols = _solve_refined(a_mats, _unit_lower_inverses(a_mats, row, col, chunk), [o["rhs"] for o in ops])
        for (ui, hh), o, qk, sol in zip(prep_probs, ops, qks, sols):
            lo = hh * HEAD_DIM
            r0 = r0s[ui]
            u_scr[pl.ds(r0, chunk), lo:lo + HEAD_DIM] = sol[:, :HEAD_DIM]
            wq_scr[pl.ds(2 * r0, 2 * chunk), lo:lo + HEAD_DIM] = jnp.concatenate(
                [sol[:, HEAD_DIM:], o["qg"]], axis=0).astype(BF16)
            qkk_scr[(units[ui][0] * n_chunks + cidx[ui]) * HEADS + hh] = jnp.concatenate(
                [qk, o["kt"].T], axis=0).astype(BF16)
        return carry

    def chain_step(c, carry):
        r0s = [_aligned(b * tile + c * chunk, chunk) for b in range(nb)]
        s_old = [s_scr[b, hh] for b, hh in chain_probs]
        ws = [_mm(wq_scr[pl.ds(2 * r0s[b], 2 * chunk), hh * HEAD_DIM:(hh + 1) * HEAD_DIM], s.astype(BF16))
              for (b, hh), s in zip(chain_probs, s_old)]
        v_new = [(u_scr[pl.ds(r0s[b], chunk), hh * HEAD_DIM:(hh + 1) * HEAD_DIM] - w[:chunk]).astype(BF16)
                 for (b, hh), w in zip(chain_probs, ws)]
        r2 = [_mm(qkk_scr[(b * n_chunks + c) * HEADS + hh], v) for (b, hh), v in zip(chain_probs, v_new)]
        for (b, hh), s, w, r in zip(chain_probs, s_old, ws, r2):
            lo = hh * HEAD_DIM
            o_scr[pl.ds(r0s[b], chunk), lo:lo + HEAD_DIM] = w[chunk:] + r[:chunk]
            glast = gc_scr[pl.ds(r0s[b] + (chunk - 1), 1), :][:, A_LANE0 + hh:A_LANE0 + hh + 1]
            s_scr[b, hh] = s * jnp.exp(glast) + r[chunk:]
        return carry

    _loop(n_chunks // cpi, prep_step)
    _loop(n_chunks, chain_step)
    snew_ref[...] = s_scr[...]

    gnw = gnw_ref[...]
    og = []
    for hh in range(HEADS):
        lo = hh * HEAD_DIM
        og.append(_rms(o_scr[:, lo:lo + HEAD_DIM], gnw) * zg_scr[:, lo:lo + HEAD_DIM])
    og = jnp.concatenate(og, axis=1).astype(BF16)
    mix = _mm(og, wo_ref[:GDN_W, :]) + _mm(sc_scr[...].astype(BF16), wo_ref[GDN_W:, :])
    y_ref[...] = (x_ref[...].reshape(rows, D_MODEL) + _rms(mix, gpost_ref[...])).reshape(nb, tile, D_MODEL)


def _ffn_kernel(x_ref, gpre_ref, wg_ref, wu_ref, wd_ref, gpost_ref, y_ref):
    x = x_ref[...]
    h = _rms(x, gpre_ref[...]).astype(BF16)
    f = None
    lo = 0
    for width in FFN_SLABS:
        act = (_silu(_mm(h, wg_ref[:, lo:lo + width])) * _mm(h, wu_ref[:, lo:lo + width])).astype(BF16)
        part = _mm(act, wd_ref[lo:lo + width, :])
        f = part if f is None else f + part
        lo += width
    y_ref[...] = x + _rms(f, gpost_ref[...])


def _const_spec(shape):
    return pl.BlockSpec(shape, lambda *_: (0,) * len(shape))


def _layer_spec(layer, shape):
    return pl.BlockSpec((None,) + shape, lambda *_: (layer, 0, 0))


def _mixer(x, conv_prev, s0, sc_prev, p, stacks, layer, *, nb, tile, chunk):
    batch, seq, _ = x.shape
    rows = nb * tile
    grid = (batch // nb, seq // tile)
    per_b = lambda i, j: (i, 0, 0)
    in_specs = [
        pl.BlockSpec((nb, tile, D_MODEL), lambda i, j: (i, j, 0)),
        pl.BlockSpec((nb, GDN_CONV - 1, QKV_W), per_b),
        pl.BlockSpec((nb, HEADS, HEAD_DIM, HEAD_DIM), lambda i, j: (i, 0, 0, 0)),
        pl.BlockSpec((nb, SC_CONV - 1, SC_W), per_b),
        _const_spec((1, D_MODEL)),
        _layer_spec(layer, (D_MODEL, QKVZ_W)),
        _layer_spec(layer, (D_MODEL, 3 * SC_W)),
        _layer_spec(layer, (D_MODEL, LANES)),
        _const_spec((GDN_CONV, QKV_W)),
        _const_spec((1, LANES)),
        _const_spec((1, LANES)),
        _const_spec((1, HEAD_DIM)),
        _const_spec((SC_CONV, SC_W)),
        _layer_spec(layer, (D_MODEL, D_MODEL)),
        _const_spec((1, D_MODEL)),
    ]
    out_specs = [
        pl.BlockSpec((nb, tile, D_MODEL), lambda i, j: (i, j, 0)),
        pl.BlockSpec((nb, GDN_CONV - 1, QKV_W), per_b),
        pl.BlockSpec((nb, HEADS, HEAD_DIM, HEAD_DIM), lambda i, j: (i, 0, 0, 0)),
        pl.BlockSpec((nb, SC_CONV - 1, SC_W), per_b),
    ]
    out_shape = [
        jax.ShapeDtypeStruct(x.shape, F32),
        jax.ShapeDtypeStruct(conv_prev.shape, F32),
        jax.ShapeDtypeStruct(s0.shape, F32),
        jax.ShapeDtypeStruct(sc_prev.shape, F32),
    ]
    scratch = [
        pltpu.VMEM((nb, HEADS, HEAD_DIM, HEAD_DIM), F32),
        pltpu.VMEM((nb, SUBLANES, QKV_W), F32),
        pltpu.VMEM((nb, SUBLANES, SC_W), F32),
        pltpu.VMEM((rows, GDN_W), F32),
        pltpu.VMEM((rows, GDN_W), F32),
        pltpu.VMEM((rows, GDN_W), F32),
        pltpu.VMEM((rows, LANES), F32),
        pltpu.VMEM((rows, LANES), F32),
        pltpu.VMEM((rows, GDN_W), F32),
        pltpu.VMEM((rows, LANES), F32),
        pltpu.VMEM((rows, GDN_W), F32),
        pltpu.VMEM((2 * rows, GDN_W), BF16),
        pltpu.VMEM((rows // chunk * HEADS, chunk + HEAD_DIM, chunk), BF16),
        pltpu.VMEM((rows, GDN_W), F32),
        pltpu.VMEM((rows, SC_W), F32),
    ]
    return pl.pallas_call(
        functools.partial(_mixer_kernel, nb=nb, tile=tile, chunk=chunk),
        grid=grid,
        in_specs=in_specs,
        out_specs=out_specs,
        out_shape=out_shape,
        scratch_shapes=scratch,
        compiler_params=pltpu.CompilerParams(
            dimension_semantics=("arbitrary", "arbitrary"),
            vmem_limit_bytes=VMEM_LIMIT_BYTES),
        name=f"mixer_t{tile}_c{chunk}",
    )(x, conv_prev, s0, sc_prev, p["norm_mix_pre"], stacks["w_in"], stacks["w_sc"], stacks["w_ba"],
      p["conv_qkv_w"], p["a_log"],
      p["dt_bias"], p["gdn_norm_w"], p["conv_sc_w"], stacks["w_o"], p["norm_mix_post"])


def _ffn(x2d, p, stacks, layer, *, tile):
    rows = x2d.shape[0]
    row_spec = pl.BlockSpec((tile, D_MODEL), lambda i: (i, 0))
    return pl.pallas_call(
        _ffn_kernel,
        grid=(rows // tile,),
        in_specs=[
            row_spec,
            _const_spec((1, D_MODEL)),
            _layer_spec(layer, (D_MODEL, D_FF)),
            _layer_spec(layer, (D_MODEL, D_FF)),
            _layer_spec(layer, (D_FF, D_MODEL)),
            _const_spec((1, D_MODEL)),
        ],
        out_specs=row_spec,
        out_shape=jax.ShapeDtypeStruct(x2d.shape, F32),
        compiler_params=pltpu.CompilerParams(
            dimension_semantics=("arbitrary",),
            vmem_limit_bytes=VMEM_LIMIT_BYTES),
        name=f"ffn_t{tile}",
    )(x2d, p["norm_ffn_pre"], stacks["w_gate"], stacks["w_up"], stacks["w_down"], p["norm_ffn_post"])


def _weight_stacks(w_in, w_o, w_gate, w_up, w_down):
    n_ba = 2 * HEADS
    w16 = w_in.astype(BF16)
    w_ba = jnp.pad(w16[..., QKVZ_W:QKVZ_W + n_ba], ((0, 0), (0, 0), (0, LANES - n_ba)))
    return {"w_in": w16[..., :QKVZ_W], "w_sc": w16[..., QKVZ_W + n_ba:], "w_ba": w_ba, "w_o": w_o.astype(BF16),
            "w_gate": w_gate.astype(BF16), "w_up": w_up.astype(BF16), "w_down": w_down.astype(BF16)}


def _layer_vectors(l, norm_mix_pre, conv_qkv_w, a_log, dt_bias, gdn_norm_w, conv_sc_w, norm_mix_post,
                   norm_ffn_pre, norm_ffn_post):
    lane_row = lambda v: jnp.zeros((1, LANES), F32).at[0, A_LANE0:A_LANE0 + HEADS].set(v.astype(F32))
    return {
        "norm_mix_pre": norm_mix_pre[l].reshape(1, D_MODEL),
        "conv_qkv_w": conv_qkv_w[l],
        "a_log": lane_row(a_log[l]),
        "dt_bias": lane_row(dt_bias[l]),
        "gdn_norm_w": gdn_norm_w[l].reshape(1, HEAD_DIM),
        "conv_sc_w": conv_sc_w[l],
        "norm_mix_post": norm_mix_post[l].reshape(1, D_MODEL),
        "norm_ffn_pre": norm_ffn_pre[l].reshape(1, D_MODEL),
        "norm_ffn_post": norm_ffn_post[l].reshape(1, D_MODEL),
    }


def _mixer_tiling(batch, seq):
    chunk = min(CHUNK, seq)
    if seq <= CHUNK:
        return batch, seq, chunk
    return batch, 256, chunk


def _run(x, conv0, s0, sc0, params, stacks):
    batch, seq, _ = x.shape
    nb, tile, chunk = _mixer_tiling(batch, seq)
    ffn_tile = min(FFN_TILE, batch * seq)
    convs, states, scs = [], [], []
    for l, p in enumerate(params):
        x, c, s, sc = _mixer(x, conv0[l], s0[l], sc0[l], p, stacks, l, nb=nb, tile=tile, chunk=chunk)
        x = _ffn(x.reshape(batch * seq, D_MODEL), p, stacks, l, tile=ffn_tile).reshape(batch, seq, D_MODEL)
        convs.append(c)
        states.append(s)
        scs.append(sc)
    return x, jnp.stack(convs), jnp.stack(states), jnp.stack(scs)


def kernel(x_prompt, x_sample, cache_gdn_conv, state_gdn, cache_sc_conv, norm_mix_pre, w_in, conv_qkv_w, a_log, dt_bias, gdn_norm_w, conv_sc_w, w_o, norm_mix_post, norm_ffn_pre, w_gate, w_up, w_down, norm_ffn_post):
    depth = w_in.shape[0]
    stacks = _weight_stacks(w_in, w_o, w_gate, w_up, w_down)
    params = [_layer_vectors(l, norm_mix_pre, conv_qkv_w, a_log, dt_bias, gdn_norm_w, conv_sc_w,
                             norm_mix_post, norm_ffn_pre, norm_ffn_post) for l in range(depth)]
    bp = x_prompt.shape[0]
    zc = jnp.zeros((depth, bp, GDN_CONV - 1, QKV_W), F32)
    zs = jnp.zeros((depth, bp, HEADS, HEAD_DIM, HEAD_DIM), F32)
    zsc = jnp.zeros((depth, bp, SC_CONV - 1, SC_W), F32)
    y_prompt, conv_p, state_p, sc_p = _run(x_prompt, zc, zs, zsc, params, stacks)
    y_sample, conv_s, state_s, sc_s = _run(x_sample, cache_gdn_conv, state_gdn, cache_sc_conv, params, stacks)
    return (y_prompt, y_sample, conv_p, state_p, sc_p, conv_s, state_s, sc_s)
```

```python
import functools

import jax
import jax.numpy as jnp
from jax import lax
from jax.experimental import pallas as pl
from jax.experimental.pallas import tpu as pltpu

F32 = jnp.float32
BF16 = jnp.bfloat16

D_MODEL = 1024
HEADS = 4
HEAD_DIM = 128
GDN_W = HEADS * HEAD_DIM
SC_W = D_MODEL - GDN_W
QKV_W = 3 * GDN_W
GDN_CONV = 4
SC_CONV = 3
D_FF = 2816
CHUNK = 64
EPS = 1e-6

QKVZ_W = QKV_W + GDN_W
LANES = 128
SUBLANES = 8
A_LANE0 = HEADS

VMEM_LIMIT_BYTES = 56 * 1024 * 1024
MXU_TILE = 256
FFN_SLABS = (3 * MXU_TILE,) * 3 + (2 * MXU_TILE,)
assert sum(FFN_SLABS) == D_FF
FFN_TILE = 1024
PREP_CHUNKS = 4
MAX_UNROLLED_TRIPS = 4


def _rms(x, g):
    return x * lax.rsqrt(jnp.mean(x * x, axis=-1, keepdims=True) + EPS) * g


def _silu(x):
    return x * jax.nn.sigmoid(x)


def _mm(a, b):
    return jnp.dot(a, b, preferred_element_type=F32)


def _aligned(index, multiple):
    return index if isinstance(index, int) else pl.multiple_of(index, multiple)


def _loop(trips, body):
    if trips <= MAX_UNROLLED_TRIPS:
        for i in range(trips):
            body(i, 0)
    else:
        lax.fori_loop(0, trips, body, 0)


def _split2(x):
    hi = x.astype(BF16)
    lo = (x - hi.astype(F32)).astype(BF16)
    return hi, lo


def _split3(x):
    h1 = x.astype(BF16)
    r1 = x - h1.astype(F32)
    h2 = r1.astype(BF16)
    h3 = (r1 - h2.astype(F32)).astype(BF16)
    return h1, h2, h3


def _unit_lower_inverses(a_strict, row, col, size):
    eye = (row == col).astype(F32)
    ts = [eye - jnp.where((row >> 1) == (col >> 1), a, 0.0) for a in a_strict]
    a16 = [a.astype(BF16) for a in a_strict]
    shift = 1
    while (1 << shift) < size:
        off = ((row >> (shift + 1)) == (col >> (shift + 1))) & ((row >> shift) != (col >> shift))
        t16 = [t.astype(BF16) for t in ts]
        inner = [_mm(jnp.where(off, a, jnp.zeros_like(a)), t) for a, t in zip(a16, t16)]
        outer = [_mm(t, i.astype(BF16)) for t, i in zip(t16, inner)]
        ts = [t - o for t, o in zip(ts, outer)]
        shift += 1
    return ts


def _solve_refined(a_strict, t_approx, rhs):
    t16 = [t.astype(BF16) for t in t_approx]
    x0 = [_mm(t, r.astype(BF16)).astype(BF16) for t, r in zip(t16, rhs)]
    a_parts = [_split2(a) for a in a_strict]
    ax_hi = [_mm(a[0], x) for a, x in zip(a_parts, x0)]
    ax_lo = [_mm(a[1], x) for a, x in zip(a_parts, x0)]
    res = [r - x.astype(F32) - (h + l) for r, x, h, l in zip(rhs, x0, ax_hi, ax_lo)]
    corr = [_mm(t, r.astype(BF16)) for t, r in zip(t16, res)]
    return [x.astype(F32) + c for x, c in zip(x0, corr)]


def _causal_dwconv(head_ref, b, x, w, tail_ref):
    width = w.shape[0]
    n = x.shape[0]
    xp = jnp.concatenate([head_ref[b], x], axis=0)
    acc = pltpu.roll(xp, width - 1, axis=0)[SUBLANES:] * w[0:1]
    for i in range(1, width - 1):
        acc = acc + pltpu.roll(xp, width - 1 - i, axis=0)[SUBLANES:] * w[i:i + 1]
    acc = acc + x * w[width - 1:width]
    head_ref[b] = xp[n:]
    tail_ref[b] = xp[n + SUBLANES - (width - 1):]
    return acc


def _mixer_kernel(x_ref, cprev_ref, s0_ref, scprev_ref, gpre_ref, wqkvz_ref, wsc_ref, wba_ref, cw_ref, alog_ref,
                  dtb_ref, gnw_ref, scw_ref, wo_ref, gpost_ref,
                  y_ref, cnew_ref, snew_ref, scnew_ref,
                  s_scr, chead, schead, q_scr, k_scr, v_scr, g_scr, beta_scr, o_scr,
                  gc_scr, u_scr, wq_scr, qkk_scr, zg_scr, sc_scr,
                  *, nb, tile, chunk):
    rows = nb * tile

    @pl.when(pl.program_id(1) == 0)
    def _():
        s_scr[...] = s0_ref[...]
        chead[...] = jnp.zeros(chead.shape, F32)
        schead[...] = jnp.zeros(schead.shape, F32)
        for b in range(nb):
            chead[b, SUBLANES - (GDN_CONV - 1):, :] = cprev_ref[b]
            schead[b, SUBLANES - (SC_CONV - 1):, :] = scprev_ref[b]

    x = x_ref[...].reshape(rows, D_MODEL)
    h = _rms(x, gpre_ref[...]).astype(BF16)

    qkv_in = _mm(h, wqkvz_ref[:, :QKV_W])
    pba = _mm(h, wba_ref[...])
    z_in = _mm(h, wqkvz_ref[:, QKV_W:])
    sc_c = _mm(h, wsc_ref[:, SC_W:2 * SC_W])
    sc_h = _mm(h, wsc_ref[:, 2 * SC_W:])
    sc_b = _mm(h, wsc_ref[:, :SC_W])

    cw = cw_ref[...]
    for b in range(nb):
        r0 = b * tile
        acc = _causal_dwconv(chead, b, qkv_in[r0:r0 + tile], cw, cnew_ref)
        act = _silu(acc)
        for hh in range(HEADS):
            lo = hh * HEAD_DIM
            qh = act[:, lo:lo + HEAD_DIM]
            kh = act[:, GDN_W + lo:GDN_W + lo + HEAD_DIM]
            q_scr[r0:r0 + tile, lo:lo + HEAD_DIM] = (
                qh * lax.rsqrt(jnp.sum(qh * qh, axis=-1, keepdims=True) + EPS) * (HEAD_DIM ** -0.5))
            k_scr[r0:r0 + tile, lo:lo + HEAD_DIM] = (
                kh * lax.rsqrt(jnp.sum(kh * kh, axis=-1, keepdims=True) + EPS))
        v_scr[r0:r0 + tile, :] = act[:, 2 * GDN_W:]

    beta_scr[...] = jax.nn.sigmoid(pba)
    ga = pba + dtb_ref[...]
    softplus = jnp.maximum(ga, 0.0) + jnp.log1p(jnp.exp(-jnp.abs(ga)))
    g_scr[...] = -jnp.exp(alog_ref[...]) * softplus

    row = lax.broadcasted_iota(jnp.int32, (chunk, chunk), 0)
    col = lax.broadcasted_iota(jnp.int32, (chunk, chunk), 1)
    causal = row >= col
    strict = row > col
    tri = causal.astype(BF16)

    zg_scr[...] = _silu(z_in)
    cm = sc_c * sc_h
    scw = scw_ref[...]
    for b in range(nb):
        r0 = b * tile
        sc_scr[r0:r0 + tile, :] = _causal_dwconv(schead, b, cm[r0:r0 + tile], scw, scnew_ref)
    sc_scr[...] = sc_b * sc_scr[...]

    n_chunks = tile // chunk
    cpi = min(PREP_CHUNKS, n_chunks)
    units = [(b, k) for b in range(nb) for k in range(cpi)]
    prep_probs = [(ui, hh) for ui in range(len(units)) for hh in range(HEADS)]
    chain_probs = [(b, hh) for b in range(nb) for hh in range(HEADS)]

    def prep_step(it, carry):
        cidx = [it * cpi + k for _, k in units]
        r0s = [_aligned(b * tile + ci * chunk, chunk) for (b, _), ci in zip(units, cidx)]
        g3 = [_split3(g_scr[pl.ds(r0, chunk), :]) for r0 in r0s]
        gcs = [_mm(tri, t[0]) + (_mm(tri, t[1]) + _mm(tri, t[2])) for t in g3]
        gcts = [gc.T for gc in gcs]
        betas = [beta_scr[pl.ds(r0, chunk), :] for r0 in r0s]
        for r0, gc in zip(r0s, gcs):
            gc_scr[pl.ds(r0, chunk), :] = gc
        ops = []
        for ui, hh in prep_probs:
            lo = hh * HEAD_DIM
            gcol = gcs[ui][:, A_LANE0 + hh:A_LANE0 + hh + 1]
            grow = gcts[ui][A_LANE0 + hh:A_LANE0 + hh + 1, :]
            bcol = betas[ui][:, hh:hh + 1]
            qh = q_scr[pl.ds(r0s[ui], chunk), lo:lo + HEAD_DIM]
            kh = k_scr[pl.ds(r0s[ui], chunk), lo:lo + HEAD_DIM]
            vh = v_scr[pl.ds(r0s[ui], chunk), lo:lo + HEAD_DIM]
            kb = kh * bcol
            eg = jnp.exp(gcol)
            k_t = kh.T
            ops.append(dict(
                decay=jnp.exp(jnp.where(causal, gcol - grow, -jnp.inf)),
                kbq16=jnp.concatenate([kb, qh], axis=0).astype(BF16), kt16=k_t.astype(BF16),
                rhs=jnp.concatenate([vh * bcol, kb * eg], axis=1),
                qg=qh * eg,
                k_tail_t=k_t * jnp.exp(gcol[chunk - 1:chunk, :] - grow)))
        kq = [_mm(o["kbq16"], o["kt16"]) for o in ops]
        a_mats = [jnp.where(strict, p[:chunk] * o["decay"], 0.0) for p, o in zip(kq, ops)]
        qks = [p[chunk:] * o["decay"] for p, o in zip(kq, ops)]
        sols = _solve_refined(a_mats, _unit_lower_inverses(a_mats, row, col, chunk), [o["rhs"] for o in ops])
        for (ui, hh), o, qk, sol in zip(prep_probs, ops, qks, sols):
            lo = hh * HEAD_DIM
            r0 = r0s[ui]
            u_scr[pl.ds(r0, chunk), lo:lo + HEAD_DIM] = sol[:, :HEAD_DIM]
            wq_scr[pl.ds(2 * r0, 2 * chunk), lo:lo + HEAD_DIM] = jnp.concatenate(
                [sol[:, HEAD_DIM:], o["qg"]], axis=0).astype(BF16)
            qkk_scr[(units[ui][0] * n_chunks + cidx[ui]) * HEADS + hh] = jnp.concatenate(
                [qk, o["k_tail_t"]], axis=0).astype(BF16)
        return carry

    def chain_step(c, carry):
        r0s = [_aligned(b * tile + c * chunk, chunk) for b in range(nb)]
        s_old = [s_scr[b, hh] for b, hh in chain_probs]
        ws = [_mm(wq_scr[pl.ds(2 * r0s[b], 2 * chunk), hh * HEAD_DIM:(hh + 1) * HEAD_DIM], s.astype(BF16))
              for (b, hh), s in zip(chain_probs, s_old)]
        v_new = [(u_scr[pl.ds(r0s[b], chunk), hh * HEAD_DIM:(hh + 1) * HEAD_DIM] - w[:chunk]).astype(BF16)
                 for (b, hh), w in zip(chain_probs, ws)]
        r2 = [_mm(qkk_scr[(b * n_chunks + c) * HEADS + hh], v) for (b, hh), v in zip(chain_probs, v_new)]
        for (b, hh), s, w, r in zip(chain_probs, s_old, ws, r2):
            lo = hh * HEAD_DIM
            o_scr[pl.ds(r0s[b], chunk), lo:lo + HEAD_DIM] = w[chunk:] + r[:chunk]
            glast = gc_scr[pl.ds(r0s[b] + (chunk - 1), 1), :][:, A_LANE0 + hh:A_LANE0 + hh + 1]
            s_scr[b, hh] = s * jnp.exp(glast) + r[chunk:]
        return carry

    _loop(n_chunks // cpi, prep_step)
    _loop(n_chunks, chain_step)
    snew_ref[...] = s_scr[...]

    gnw = gnw_ref[...]
    og = []
    for hh in range(HEADS):
        lo = hh * HEAD_DIM
        og.append(_rms(o_scr[:, lo:lo + HEAD_DIM], gnw) * zg_scr[:, lo:lo + HEAD_DIM])
    og = jnp.concatenate(og, axis=1).astype(BF16)
    mix = _mm(og, wo_ref[:GDN_W, :]) + _mm(sc_scr[...].astype(BF16), wo_ref[GDN_W:, :])
    y_ref[...] = (x_ref[...].reshape(rows, D_MODEL) + _rms(mix, gpost_ref[...])).reshape(nb, tile, D_MODEL)


def _ffn_kernel(x_ref, gpre_ref, wg_ref, wu_ref, wd_ref, gpost_ref, y_ref):
    x = x_ref[...]
    h = _rms(x, gpre_ref[...]).astype(BF16)
    f = None
    lo = 0
    for width in FFN_SLABS:
        act = (_silu(_mm(h, wg_ref[:, lo:lo + width])) * _mm(h, wu_ref[:, lo:lo + width])).astype(BF16)
        part = _mm(act, wd_ref[lo:lo + width, :])
        f = part if f is None else f + part
        lo += width
    y_ref[...] = x + _rms(f, gpost_ref[...])


def _const_spec(shape):
    return pl.BlockSpec(shape, lambda *_: (0,) * len(shape))


def _layer_spec(layer, shape):
    return pl.BlockSpec((None,) + shape, lambda *_: (layer, 0, 0))


def _mixer(x, conv_prev, s0, sc_prev, p, stacks, layer, *, nb, tile, chunk):
    batch, seq, _ = x.shape
    rows = nb * tile
    grid = (batch // nb, seq // tile)
    per_b = lambda i, j: (i, 0, 0)
    in_specs = [
        pl.BlockSpec((nb, tile, D_MODEL), lambda i, j: (i, j, 0)),
        pl.BlockSpec((nb, GDN_CONV - 1, QKV_W), per_b),
        pl.BlockSpec((nb, HEADS, HEAD_DIM, HEAD_DIM), lambda i, j: (i, 0, 0, 0)),
        pl.BlockSpec((nb, SC_CONV - 1, SC_W), per_b),
        _const_spec((1, D_MODEL)),
        _layer_spec(layer, (D_MODEL, QKVZ_W)),
        _layer_spec(layer, (D_MODEL, 3 * SC_W)),
        _layer_spec(layer, (D_MODEL, LANES)),
        _const_spec((GDN_CONV, QKV_W)),
        _const_spec((1, LANES)),
        _const_spec((1, LANES)),
        _const_spec((1, HEAD_DIM)),
        _const_spec((SC_CONV, SC_W)),
        _layer_spec(layer, (D_MODEL, D_MODEL)),
        _const_spec((1, D_MODEL)),
    ]
    out_specs = [
        pl.BlockSpec((nb, tile, D_MODEL), lambda i, j: (i, j, 0)),
        pl.BlockSpec((nb, GDN_CONV - 1, QKV_W), per_b),
        pl.BlockSpec((nb, HEADS, HEAD_DIM, HEAD_DIM), lambda i, j: (i, 0, 0, 0)),
        pl.BlockSpec((nb, SC_CONV - 1, SC_W), per_b),
    ]
    out_shape = [
        jax.ShapeDtypeStruct(x.shape, F32),
        jax.ShapeDtypeStruct(conv_prev.shape, F32),
        jax.ShapeDtypeStruct(s0.shape, F32),
        jax.ShapeDtypeStruct(sc_prev.shape, F32),
    ]
    scratch = [
        pltpu.VMEM((nb, HEADS, HEAD_DIM, HEAD_DIM), F32),
        pltpu.VMEM((nb, SUBLANES, QKV_W), F32),
        pltpu.VMEM((nb, SUBLANES, SC_W), F32),
        pltpu.VMEM((rows, GDN_W), F32),
        pltpu.VMEM((rows, GDN_W), F32),
        pltpu.VMEM((rows, GDN_W), F32),
        pltpu.VMEM((rows, LANES), F32),
        pltpu.VMEM((rows, LANES), F32),
        pltpu.VMEM((rows, GDN_W), F32),
        pltpu.VMEM((rows, LANES), F32),
        pltpu.VMEM((rows, GDN_W), F32),
        pltpu.VMEM((2 * rows, GDN_W), BF16),
        pltpu.VMEM((rows // chunk * HEADS, chunk + HEAD_DIM, chunk), BF16),
        pltpu.VMEM((rows, GDN_W), F32),
        pltpu.VMEM((rows, SC_W), F32),
    ]
    return pl.pallas_call(
        functools.partial(_mixer_kernel, nb=nb, tile=tile, chunk=chunk),
        grid=grid,
        in_specs=in_specs,
        out_specs=out_specs,
        out_shape=out_shape,
        scratch_shapes=scratch,
        compiler_params=pltpu.CompilerParams(
            dimension_semantics=("arbitrary", "arbitrary"),
            vmem_limit_bytes=VMEM_LIMIT_BYTES),
        name=f"mixer_t{tile}_c{chunk}",
    )(x, conv_prev, s0, sc_prev, p["norm_mix_pre"], stacks["w_in"], stacks["w_sc"], stacks["w_ba"],
      p["conv_qkv_w"], p["a_log"],
      p["dt_bias"], p["gdn_norm_w"], p["conv_sc_w"], stacks["w_o"], p["norm_mix_post"])


def _ffn(x2d, p, stacks, layer, *, tile):
    rows = x2d.shape[0]
    row_spec = pl.BlockSpec((tile, D_MODEL), lambda i: (i, 0))
    return pl.pallas_call(
        _ffn_kernel,
        grid=(rows // tile,),
        in_specs=[
            row_spec,
            _const_spec((1, D_MODEL)),
            _layer_spec(layer, (D_MODEL, D_FF)),
            _layer_spec(layer, (D_MODEL, D_FF)),
            _layer_spec(layer, (D_FF, D_MODEL)),
            _const_spec((1, D_MODEL)),
        ],
        out_specs=row_spec,
        out_shape=jax.ShapeDtypeStruct(x2d.shape, F32),
        compiler_params=pltpu.CompilerParams(
            dimension_semantics=("arbitrary",),
            vmem_limit_bytes=VMEM_LIMIT_BYTES),
        name=f"ffn_t{tile}",
    )(x2d, p["norm_ffn_pre"], stacks["w_gate"], stacks["w_up"], stacks["w_down"], p["norm_ffn_post"])


def _weight_stacks(w_in, w_o, w_gate, w_up, w_down):
    n_ba = 2 * HEADS
    w16 = w_in.astype(BF16)
    w_ba = jnp.pad(w16[..., QKVZ_W:QKVZ_W + n_ba], ((0, 0), (0, 0), (0, LANES - n_ba)))
    return {"w_in": w16[..., :QKVZ_W], "w_sc": w16[..., QKVZ_W + n_ba:], "w_ba": w_ba, "w_o": w_o.astype(BF16),
            "w_gate": w_gate.astype(BF16), "w_up": w_up.astype(BF16), "w_down": w_down.astype(BF16)}


def _layer_vectors(l, norm_mix_pre, conv_qkv_w, a_log, dt_bias, gdn_norm_w, conv_sc_w, norm_mix_post,
                   norm_ffn_pre, norm_ffn_post):
    lane_row = lambda v: jnp.zeros((1, LANES), F32).at[0, A_LANE0:A_LANE0 + HEADS].set(v.astype(F32))
    return {
        "norm_mix_pre": norm_mix_pre[l].reshape(1, D_MODEL),
        "conv_qkv_w": conv_qkv_w[l],
        "a_log": lane_row(a_log[l]),
        "dt_bias": lane_row(dt_bias[l]),
        "gdn_norm_w": gdn_norm_w[l].reshape(1, HEAD_DIM),
        "conv_sc_w": conv_sc_w[l],
        "norm_mix_post": norm_mix_post[l].reshape(1, D_MODEL),
        "norm_ffn_pre": norm_ffn_pre[l].reshape(1, D_MODEL),
        "norm_ffn_post": norm_ffn_post[l].reshape(1, D_MODEL),
    }


def _mixer_tiling(batch, seq):
    chunk = min(CHUNK, seq)
    if seq <= CHUNK:
        return batch, seq, chunk
    return batch, 256, chunk


def _run(x, conv0, s0, sc0, params, stacks):
    batch, seq, _ = x.shape
    nb, tile, chunk = _mixer_tiling(batch, seq)
    ffn_tile = min(FFN_TILE, batch * seq)
    convs, states, scs = [], [], []
    for l, p in enumerate(params):
        x, c, s, sc = _mixer(x, conv0[l], s0[l], sc0[l], p, stacks, l, nb=nb, tile=tile, chunk=chunk)
        x = _ffn(x.reshape(batch * seq, D_MODEL), p, stacks, l, tile=ffn_tile).reshape(batch, seq, D_MODEL)
        convs.append(c)
        states.append(s)
        scs.append(sc)
    return x, jnp.stack(convs), jnp.stack(states), jnp.stack(scs)


def kernel(x_prompt, x_sample, cache_gdn_conv, state_gdn, cache_sc_conv, norm_mix_pre, w_in, conv_qkv_w, a_log, dt_bias, gdn_norm_w, conv_sc_w, w_o, norm_mix_post, norm_ffn_pre, w_gate, w_up, w_down, norm_ffn_post):
    depth = w_in.shape[0]
    stacks = _weight_stacks(w_in, w_o, w_gate, w_up, w_down)
    params = [_layer_vectors(l, norm_mix_pre, conv_qkv_w, a_log, dt_bias, gdn_norm_w, conv_sc_w,
                             norm_mix_post, norm_ffn_pre, norm_ffn_post) for l in range(depth)]
    bp = x_prompt.shape[0]
    zc = jnp.zeros((depth, bp, GDN_CONV - 1, QKV_W), F32)
    zs = jnp.zeros((depth, bp, HEADS, HEAD_DIM, HEAD_DIM), F32)
    zsc = jnp.zeros((depth, bp, SC_CONV - 1, SC_W), F32)
    y_prompt, conv_p, state_p, sc_p = _run(x_prompt, zc, zs, zsc, params, stacks)
    y_sample, conv_s, state_s, sc_s = _run(x_sample, cache_gdn_conv, state_gdn, cache_sc_conv, params, stacks)
    return (y_prompt, y_sample, conv_p, state_p, sc_p, conv_s, state_s, sc_s)
```

```python
import functools

import jax
import jax.numpy as jnp
from jax import lax
from jax.experimental import pallas as pl
from jax.experimental.pallas import tpu as pltpu

F32 = jnp.float32
BF16 = jnp.bfloat16

D_MODEL = 1024
HEADS = 4
HEAD_DIM = 128
GDN_W = HEADS * HEAD_DIM
SC_W = D_MODEL - GDN_W
QKV_W = 3 * GDN_W
GDN_CONV = 4
SC_CONV = 3
D_FF = 2816
CHUNK = 64
EPS = 1e-6

QKVZ_W = QKV_W + GDN_W
LANES = 128
SUBLANES = 8
A_LANE0 = HEADS

VMEM_LIMIT_BYTES = 56 * 1024 * 1024
MXU_TILE = 256
FFN_SLABS = (3 * MXU_TILE,) * 3 + (2 * MXU_TILE,)
assert sum(FFN_SLABS) == D_FF
FFN_TILE = 1024
PREP_CHUNKS = 4
MAX_UNROLLED_TRIPS = 4


def _rms(x, g):
    return x * lax.rsqrt(jnp.mean(x * x, axis=-1, keepdims=True) + EPS) * g


def _silu(x):
    return x * jax.nn.sigmoid(x)


def _mm(a, b):
    return jnp.dot(a, b, preferred_element_type=F32)


def _aligned(index, multiple):
    return index if isinstance(index, int) else pl.multiple_of(index, multiple)


def _loop(trips, body):
    if trips <= MAX_UNROLLED_TRIPS:
        for i in range(trips):
            body(i, 0)
    else:
        lax.fori_loop(0, trips, body, 0)


def _split2(x):
    hi = x.astype(BF16)
    lo = (x - hi.astype(F32)).astype(BF16)
    return hi, lo


def _split3(x):
    h1 = x.astype(BF16)
    r1 = x - h1.astype(F32)
    h2 = r1.astype(BF16)
    h3 = (r1 - h2.astype(F32)).astype(BF16)
    return h1, h2, h3


def _unit_lower_inverses(a_strict, row, col, size):
    eye = (row == col).astype(F32)
    ts = [eye - jnp.where((row >> 1) == (col >> 1), a, 0.0) for a in a_strict]
    a16 = [a.astype(BF16) for a in a_strict]
    shift = 1
    while (1 << shift) < size:
        off = ((row >> (shift + 1)) == (col >> (shift + 1))) & ((row >> shift) != (col >> shift))
        t16 = [t.astype(BF16) for t in ts]
        inner = [_mm(jnp.where(off, a, jnp.zeros_like(a)), t) for a, t in zip(a16, t16)]
        outer = [_mm(t, i.astype(BF16)) for t, i in zip(t16, inner)]
        ts = [t - o for t, o in zip(ts, outer)]
        shift += 1
    return ts


def _solve_refined(a_strict, t_approx, rhs):
    t16 = [t.astype(BF16) for t in t_approx]
    x0 = [_mm(t, r.astype(BF16)).astype(BF16) for t, r in zip(t16, rhs)]
    a_parts = [_split2(a) for a in a_strict]
    ax_hi = [_mm(a[0], x) for a, x in zip(a_parts, x0)]
    ax_lo = [_mm(a[1], x) for a, x in zip(a_parts, x0)]
    res = [r - x.astype(F32) - (h + l) for r, x, h, l in zip(rhs, x0, ax_hi, ax_lo)]
    corr = [_mm(t, r.astype(BF16)) for t, r in zip(t16, res)]
    return [x.astype(F32) + c for x, c in zip(x0, corr)]


def _causal_dwconv(head_ref, b, x, w, tail_ref):
    width = w.shape[0]
    n = x.shape[0]
    xp = jnp.concatenate([head_ref[b], x], axis=0)
    acc = pltpu.roll(xp, width - 1, axis=0)[SUBLANES:] * w[0:1]
    for i in range(1, width - 1):
        acc = acc + pltpu.roll(xp, width - 1 - i, axis=0)[SUBLANES:] * w[i:i + 1]
    acc = acc + x * w[width - 1:width]
    head_ref[b] = xp[n:]
    tail_ref[b] = xp[n + SUBLANES - (width - 1):]
    return acc


def _mixer_kernel(x_ref, cprev_ref, s0_ref, scprev_ref, gpre_ref, wqkvz_ref, wsc_ref, wba_ref, cw_ref, alog_ref,
                  dtb_ref, gnw_ref, scw_ref, wo_ref, gpost_ref,
                  y_ref, cnew_ref, snew_ref, scnew_ref,
                  s_scr, chead, schead, q_scr, k_scr, v_scr, g_scr, beta_scr, o_scr,
                  gc_scr, u_scr, wq_scr, qkk_scr, zg_scr, sc_scr,
                  *, nb, tile, chunk):
    rows = nb * tile

    @pl.when(pl.program_id(1) == 0)
    def _():
        s_scr[...] = s0_ref[...]
        chead[...] = jnp.zeros(chead.shape, F32)
        schead[...] = jnp.zeros(schead.shape, F32)
        for b in range(nb):
            chead[b, SUBLANES - (GDN_CONV - 1):, :] = cprev_ref[b]
            schead[b, SUBLANES - (SC_CONV - 1):, :] = scprev_ref[b]

    x = x_ref[...].reshape(rows, D_MODEL)
    h = _rms(x, gpre_ref[...]).astype(BF16)

    qkv_in = _mm(h, wqkvz_ref[:, :QKV_W])
    pba = _mm(h, wba_ref[...])
    z_in = _mm(h, wqkvz_ref[:, QKV_W:])
    sc_c = _mm(h, wsc_ref[:, SC_W:2 * SC_W])
    sc_h = _mm(h, wsc_ref[:, 2 * SC_W:])
    sc_b = _mm(h, wsc_ref[:, :SC_W])

    cw = cw_ref[...]
    for b in range(nb):
        r0 = b * tile
        acc = _causal_dwconv(chead, b, qkv_in[r0:r0 + tile], cw, cnew_ref)
        act = _silu(acc)
        for hh in range(HEADS):
            lo = hh * HEAD_DIM
            qh = act[:, lo:lo + HEAD_DIM]
            kh = act[:, GDN_W + lo:GDN_W + lo + HEAD_DIM]
            q_scr[r0:r0 + tile, lo:lo + HEAD_DIM] = (
                qh * lax.rsqrt(jnp.sum(qh * qh, axis=-1, keepdims=True) + EPS) * (HEAD_DIM ** -0.5))
            k_scr[r0:r0 + tile, lo:lo + HEAD_DIM] = (
                kh * lax.rsqrt(jnp.sum(kh * kh, axis=-1, keepdims=True) + EPS))
        v_scr[r0:r0 + tile, :] = act[:, 2 * GDN_W:]

    beta_scr[...] = jax.nn.sigmoid(pba)
    ga = pba + dtb_ref[...]
    softplus = jnp.maximum(ga, 0.0) + jnp.log1p(jnp.exp(-jnp.abs(ga)))
    g_scr[...] = -jnp.exp(alog_ref[...]) * softplus

    row = lax.broadcasted_iota(jnp.int32, (chunk, chunk), 0)
    col = lax.broadcasted_iota(jnp.int32, (chunk, chunk), 1)
    causal = row >= col
    strict = row > col
    tri = causal.astype(BF16)

    zg_scr[...] = _silu(z_in)
    cm = sc_c * sc_h
    scw = scw_ref[...]
    for b in range(nb):
        r0 = b * tile
        sc_scr[r0:r0 + tile, :] = sc_b[r0:r0 + tile] * _causal_dwconv(schead, b, cm[r0:r0 + tile], scw, scnew_ref)

    n_chunks = tile // chunk
    cpi = min(PREP_CHUNKS, n_chunks)
    units = [(b, k) for b in range(nb) for k in range(cpi)]
    prep_probs = [(ui, hh) for ui in range(len(units)) for hh in range(HEADS)]
    chain_probs = [(b, hh) for b in range(nb) for hh in range(HEADS)]

    def prep_step(it, carry):
        cidx = [it * cpi + k for _, k in units]
        r0s = [_aligned(b * tile + ci * chunk, chunk) for (b, _), ci in zip(units, cidx)]
        g3 = [_split3(g_scr[pl.ds(r0, chunk), :]) for r0 in r0s]
        gcs = [_mm(tri, t[0]) + (_mm(tri, t[1]) + _mm(tri, t[2])) for t in g3]
        gcts = [gc.T for gc in gcs]
        betas = [beta_scr[pl.ds(r0, chunk), :] for r0 in r0s]
        for r0, gc in zip(r0s, gcs):
            gc_scr[pl.ds(r0, chunk), :] = gc
        ops = []
        for ui, hh in prep_probs:
            lo = hh * HEAD_DIM
            gcol = gcs[ui][:, A_LANE0 + hh:A_LANE0 + hh + 1]
            grow = gcts[ui][A_LANE0 + hh:A_LANE0 + hh + 1, :]
            bcol = betas[ui][:, hh:hh + 1]
            qh = q_scr[pl.ds(r0s[ui], chunk), lo:lo + HEAD_DIM]
            kh = k_scr[pl.ds(r0s[ui], chunk), lo:lo + HEAD_DIM]
            vh = v_scr[pl.ds(r0s[ui], chunk), lo:lo + HEAD_DIM]
            kb = kh * bcol
            eg = jnp.exp(gcol)
            k_t = kh.T
            ops.append(dict(
                decay=jnp.exp(jnp.where(causal, gcol - grow, -jnp.inf)),
                kbq16=jnp.concatenate([kb, qh], axis=0).astype(BF16), kt16=k_t.astype(BF16),
                rhs=jnp.concatenate([vh * bcol, kb * eg], axis=1),
                qg=qh * eg,
                k_tail_t=k_t * jnp.exp(gcol[chunk - 1:chunk, :] - grow)))
        kq = [_mm(o["kbq16"], o["kt16"]) for o in ops]
        a_mats = [jnp.where(strict, p[:chunk] * o["decay"], 0.0) for p, o in zip(kq, ops)]
        qks = [p[chunk:] * o["decay"] for p, o in zip(kq, ops)]
        sols = _solve_refined(a_mats, _unit_lower_inverses(a_mats, row, col, chunk), [o["rhs"] for o in ops])
        for (ui, hh), o, qk, sol in zip(prep_probs, ops, qks, sols):
            lo = hh * HEAD_DIM
            r0 = r0s[ui]
            u_scr[pl.ds(r0, chunk), lo:lo + HEAD_DIM] = sol[:, :HEAD_DIM]
            wq_scr[pl.ds(2 * r0, 2 * chunk), lo:lo + HEAD_DIM] = jnp.concatenate(
                [sol[:, HEAD_DIM:], o["qg"]], axis=0).astype(BF16)
            qkk_scr[(units[ui][0] * n_chunks + cidx[ui]) * HEADS + hh] = jnp.concatenate(
                [qk, o["k_tail_t"]], axis=0).astype(BF16)
        return carry

    def chain_step(c, carry):
        r0s = [_aligned(b * tile + c * chunk, chunk) for b in range(nb)]
        s_old = [s_scr[b, hh] for b, hh in chain_probs]
        ws = [_mm(wq_scr[pl.ds(2 * r0s[b], 2 * chunk), hh * HEAD_DIM:(hh + 1) * HEAD_DIM], s.astype(BF16))
              for (b, hh), s in zip(chain_probs, s_old)]
        v_new = [(u_scr[pl.ds(r0s[b], chunk), hh * HEAD_DIM:(hh + 1) * HEAD_DIM] - w[:chunk]).astype(BF16)
                 for (b, hh), w in zip(chain_probs, ws)]
        r2 = [_mm(qkk_scr[(b * n_chunks + c) * HEADS + hh], v) for (b, hh), v in zip(chain_probs, v_new)]
        for (b, hh), s, w, r in zip(chain_probs, s_old, ws, r2):
            lo = hh * HEAD_DIM
            o_scr[pl.ds(r0s[b], chunk), lo:lo + HEAD_DIM] = w[chunk:] + r[:chunk]
            glast = gc_scr[pl.ds(r0s[b] + (chunk - 1), 1), :][:, A_LANE0 + hh:A_LANE0 + hh + 1]
            s_scr[b, hh] = s * jnp.exp(glast) + r[chunk:]
        return carry

    _loop(n_chunks // cpi, prep_step)
    _loop(n_chunks, chain_step)
    snew_ref[...] = s_scr[...]

    gnw = gnw_ref[...]
    og = []
    for hh in range(HEADS):
        lo = hh * HEAD_DIM
        og.append(_rms(o_scr[:, lo:lo + HEAD_DIM], gnw) * zg_scr[:, lo:lo + HEAD_DIM])
    og = jnp.concatenate(og, axis=1).astype(BF16)
    mix = _mm(og, wo_ref[:GDN_W, :]) + _mm(sc_scr[...].astype(BF16), wo_ref[GDN_W:, :])
    y_ref[...] = (x_ref[...].reshape(rows, D_MODEL) + _rms(mix, gpost_ref[...])).reshape(nb, tile, D_MODEL)


def _ffn_kernel(x_ref, gpre_ref, wg_ref, wu_ref, wd_ref, gpost_ref, y_ref):
    x = x_ref[...]
    h = _rms(x, gpre_ref[...]).astype(BF16)
    f = None
    lo = 0
    for width in FFN_SLABS:
        act = (_silu(_mm(h, wg_ref[:, lo:lo + width])) * _mm(h, wu_ref[:, lo:lo + width])).astype(BF16)
        part = _mm(act, wd_ref[lo:lo + width, :])
        f = part if f is None else f + part
        lo += width
    y_ref[...] = x + _rms(f, gpost_ref[...])


def _const_spec(shape):
    return pl.BlockSpec(shape, lambda *_: (0,) * len(shape))


def _layer_spec(layer, shape):
    return pl.BlockSpec((None,) + shape, lambda *_: (layer, 0, 0))


def _mixer(x, conv_prev, s0, sc_prev, p, stacks, layer, *, nb, tile, chunk):
    batch, seq, _ = x.shape
    rows = nb * tile
    grid = (batch // nb, seq // tile)
    per_b = lambda i, j: (i, 0, 0)
    in_specs = [
        pl.BlockSpec((nb, tile, D_MODEL), lambda i, j: (i, j, 0)),
        pl.BlockSpec((None, nb, GDN_CONV - 1, QKV_W), lambda i, j: (layer, i, 0, 0)),
        pl.BlockSpec((None, nb, HEADS, HEAD_DIM, HEAD_DIM), lambda i, j: (layer, i, 0, 0, 0)),
        pl.BlockSpec((None, nb, SC_CONV - 1, SC_W), lambda i, j: (layer, i, 0, 0)),
        _const_spec((1, D_MODEL)),
        _layer_spec(layer, (D_MODEL, QKVZ_W)),
        _layer_spec(layer, (D_MODEL, 3 * SC_W)),
        _layer_spec(layer, (D_MODEL, LANES)),
        _const_spec((GDN_CONV, QKV_W)),
        _const_spec((1, LANES)),
        _const_spec((1, LANES)),
        _const_spec((1, HEAD_DIM)),
        _const_spec((SC_CONV, SC_W)),
        _layer_spec(layer, (D_MODEL, D_MODEL)),
        _const_spec((1, D_MODEL)),
    ]
    out_specs = [
        pl.BlockSpec((nb, tile, D_MODEL), lambda i, j: (i, j, 0)),
        pl.BlockSpec((nb, GDN_CONV - 1, QKV_W), per_b),
        pl.BlockSpec((nb, HEADS, HEAD_DIM, HEAD_DIM), lambda i, j: (i, 0, 0, 0)),
        pl.BlockSpec((nb, SC_CONV - 1, SC_W), per_b),
    ]
    out_shape = [
        jax.ShapeDtypeStruct(x.shape, F32),
        jax.ShapeDtypeStruct(conv_prev.shape[1:], F32),
        jax.ShapeDtypeStruct(s0.shape[1:], F32),
        jax.ShapeDtypeStruct(sc_prev.shape[1:], F32),
    ]
    scratch = [
        pltpu.VMEM((nb, HEADS, HEAD_DIM, HEAD_DIM), F32),
        pltpu.VMEM((nb, SUBLANES, QKV_W), F32),
        pltpu.VMEM((nb, SUBLANES, SC_W), F32),
        pltpu.VMEM((rows, GDN_W), F32),
        pltpu.VMEM((rows, GDN_W), F32),
        pltpu.VMEM((rows, GDN_W), F32),
        pltpu.VMEM((rows, LANES), F32),
        pltpu.VMEM((rows, LANES), F32),
        pltpu.VMEM((rows, GDN_W), F32),
        pltpu.VMEM((rows, LANES), F32),
        pltpu.VMEM((rows, GDN_W), F32),
        pltpu.VMEM((2 * rows, GDN_W), BF16),
        pltpu.VMEM((rows // chunk * HEADS, chunk + HEAD_DIM, chunk), BF16),
        pltpu.VMEM((rows, GDN_W), F32),
        pltpu.VMEM((rows, SC_W), F32),
    ]
    return pl.pallas_call(
        functools.partial(_mixer_kernel, nb=nb, tile=tile, chunk=chunk),
        grid=grid,
        in_specs=in_specs,
        out_specs=out_specs,
        out_shape=out_shape,
        scratch_shapes=scratch,
        compiler_params=pltpu.CompilerParams(
            dimension_semantics=("arbitrary", "arbitrary"),
            vmem_limit_bytes=VMEM_LIMIT_BYTES),
        name=f"mixer_t{tile}_c{chunk}",
    )(x, conv_prev, s0, sc_prev, p["norm_mix_pre"], stacks["w_in"], stacks["w_sc"], stacks["w_ba"],
      p["conv_qkv_w"], p["a_log"],
      p["dt_bias"], p["gdn_norm_w"], p["conv_sc_w"], stacks["w_o"], p["norm_mix_post"])


def _ffn(x2d, p, stacks, layer, *, tile):
    rows = x2d.shape[0]
    row_spec = pl.BlockSpec((tile, D_MODEL), lambda i: (i, 0))
    return pl.pallas_call(
        _ffn_kernel,
        grid=(rows // tile,),
        in_specs=[
            row_spec,
            _const_spec((1, D_MODEL)),
            _layer_spec(layer, (D_MODEL, D_FF)),
            _layer_spec(layer, (D_MODEL, D_FF)),
            _layer_spec(layer, (D_FF, D_MODEL)),
            _const_spec((1, D_MODEL)),
        ],
        out_specs=row_spec,
        out_shape=jax.ShapeDtypeStruct(x2d.shape, F32),
        compiler_params=pltpu.CompilerParams(
            dimension_semantics=("arbitrary",),
            vmem_limit_bytes=VMEM_LIMIT_BYTES),
        name=f"ffn_t{tile}",
    )(x2d, p["norm_ffn_pre"], stacks["w_gate"], stacks["w_up"], stacks["w_down"], p["norm_ffn_post"])


def _weight_stacks(w_in, w_o, w_gate, w_up, w_down):
    n_ba = 2 * HEADS
    w_ba = jnp.pad(w_in[..., QKVZ_W:QKVZ_W + n_ba].astype(BF16), ((0, 0), (0, 0), (0, LANES - n_ba)))
    return {"w_in": w_in[..., :QKVZ_W].astype(BF16), "w_sc": w_in[..., QKVZ_W + n_ba:].astype(BF16),
            "w_ba": w_ba, "w_o": w_o.astype(BF16),
            "w_gate": w_gate.astype(BF16), "w_up": w_up.astype(BF16), "w_down": w_down.astype(BF16)}


def _layer_vectors(l, norm_mix_pre, conv_qkv_w, a_log, dt_bias, gdn_norm_w, conv_sc_w, norm_mix_post,
                   norm_ffn_pre, norm_ffn_post):
    lane_row = lambda v: jnp.zeros((1, LANES), F32).at[0, A_LANE0:A_LANE0 + HEADS].set(v.astype(F32))
    return {
        "norm_mix_pre": norm_mix_pre[l].reshape(1, D_MODEL),
        "conv_qkv_w": conv_qkv_w[l],
        "a_log": lane_row(a_log[l]),
        "dt_bias": lane_row(dt_bias[l]),
        "gdn_norm_w": gdn_norm_w[l].reshape(1, HEAD_DIM),
        "conv_sc_w": conv_sc_w[l],
        "norm_mix_post": norm_mix_post[l].reshape(1, D_MODEL),
        "norm_ffn_pre": norm_ffn_pre[l].reshape(1, D_MODEL),
        "norm_ffn_post": norm_ffn_post[l].reshape(1, D_MODEL),
    }


def _mixer_tiling(batch, seq):
    chunk = min(CHUNK, seq)
    if seq <= CHUNK:
        return batch, seq, chunk
    return batch, 256, chunk


def _run(x, conv0, s0, sc0, params, stacks):
    batch, seq, _ = x.shape
    nb, tile, chunk = _mixer_tiling(batch, seq)
    ffn_tile = min(FFN_TILE, batch * seq)
    convs, states, scs = [], [], []
    for l, p in enumerate(params):
        x, c, s, sc = _mixer(x, conv0, s0, sc0, p, stacks, l, nb=nb, tile=tile, chunk=chunk)
        x = _ffn(x.reshape(batch * seq, D_MODEL), p, stacks, l, tile=ffn_tile).reshape(batch, seq, D_MODEL)
        convs.append(c)
        states.append(s)
        scs.append(sc)
    return x, jnp.stack(convs), jnp.stack(states), jnp.stack(scs)


def kernel(x_prompt, x_sample, cache_gdn_conv, state_gdn, cache_sc_conv, norm_mix_pre, w_in, conv_qkv_w, a_log, dt_bias, gdn_norm_w, conv_sc_w, w_o, norm_mix_post, norm_ffn_pre, w_gate, w_up, w_down, norm_ffn_post):
    depth = w_in.shape[0]
    stacks = _weight_stacks(w_in, w_o, w_gate, w_up, w_down)
    params = [_layer_vectors(l, norm_mix_pre, conv_qkv_w, a_log, dt_bias, gdn_norm_w, conv_sc_w,
                             norm_mix_post, norm_ffn_pre, norm_ffn_post) for l in range(depth)]
    bp = x_prompt.shape[0]
    zc = jnp.zeros((depth, bp, GDN_CONV - 1, QKV_W), F32)
    zs = jnp.zeros((depth, bp, HEADS, HEAD_DIM, HEAD_DIM), F32)
    zsc = jnp.zeros((depth, bp, SC_CONV - 1, SC_W), F32)
    y_prompt, conv_p, state_p, sc_p = _run(x_prompt, zc, zs, zsc, params, stacks)
    y_sample, conv_s, state_s, sc_s = _run(x_sample, cache_gdn_conv, state_gdn, cache_sc_conv, params, stacks)
    return (y_prompt, y_sample, conv_p, state_p, sc_p, conv_s, state_s, sc_s)
```

```python
import functools

import jax
import jax.numpy as jnp
from jax import lax
from jax.experimental import pallas as pl
from jax.experimental.pallas import tpu as pltpu

F32 = jnp.float32
BF16 = jnp.bfloat16

D_MODEL = 1024
HEADS = 4
HEAD_DIM = 128
GDN_W = HEADS * HEAD_DIM
SC_W = D_MODEL - GDN_W
QKV_W = 3 * GDN_W
GDN_CONV = 4
SC_CONV = 3
D_FF = 2816
CHUNK = 64
EPS = 1e-6

QKVZ_W = QKV_W + GDN_W
LANES = 128
SUBLANES = 8
A_LANE0 = HEADS

VMEM_LIMIT_BYTES = 56 * 1024 * 1024
MXU_TILE = 256
FFN_SLABS = (3 * MXU_TILE,) * 3 + (2 * MXU_TILE,)
assert sum(FFN_SLABS) == D_FF
FFN_TILE = 1024
PREP_CHUNKS = 4
MAX_UNROLLED_TRIPS = 8


def _rms(x, g):
    return x * lax.rsqrt(jnp.mean(x * x, axis=-1, keepdims=True) + EPS) * g


def _silu(x):
    return x * jax.nn.sigmoid(x)


def _mm(a, b):
    return jnp.dot(a, b, preferred_element_type=F32)


def _aligned(index, multiple):
    return index if isinstance(index, int) else pl.multiple_of(index, multiple)


def _loop(trips, body):
    if trips <= MAX_UNROLLED_TRIPS:
        for i in range(trips):
            body(i, 0)
    else:
        lax.fori_loop(0, trips, body, 0)


def _split2(x):
    hi = x.astype(BF16)
    lo = (x - hi.astype(F32)).astype(BF16)
    return hi, lo


def _split3(x):
    h1 = x.astype(BF16)
    r1 = x - h1.astype(F32)
    h2 = r1.astype(BF16)
    h3 = (r1 - h2.astype(F32)).astype(BF16)
    return h1, h2, h3


def _unit_lower_inverses(a_strict, row, col, size):
    eye = (row == col).astype(F32)
    ts = [eye - jnp.where((row >> 1) == (col >> 1), a, 0.0) for a in a_strict]
    a16 = [a.astype(BF16) for a in a_strict]
    shift = 1
    while (1 << shift) < size:
        off = ((row >> (shift + 1)) == (col >> (shift + 1))) & ((row >> shift) != (col >> shift))
        t16 = [t.astype(BF16) for t in ts]
        inner = [_mm(jnp.where(off, a, jnp.zeros_like(a)), t) for a, t in zip(a16, t16)]
        outer = [_mm(t, i.astype(BF16)) for t, i in zip(t16, inner)]
        ts = [t - o for t, o in zip(ts, outer)]
        shift += 1
    return ts


def _solve_refined(a_strict, t_approx, rhs):
    t16 = [t.astype(BF16) for t in t_approx]
    x0 = [_mm(t, r.astype(BF16)).astype(BF16) for t, r in zip(t16, rhs)]
    a_parts = [_split2(a) for a in a_strict]
    ax_hi = [_mm(a[0], x) for a, x in zip(a_parts, x0)]
    ax_lo = [_mm(a[1], x) for a, x in zip(a_parts, x0)]
    res = [r - x.astype(F32) - (h + l) for r, x, h, l in zip(rhs, x0, ax_hi, ax_lo)]
    corr = [_mm(t, r.astype(BF16)) for t, r in zip(t16, res)]
    return [x.astype(F32) + c for x, c in zip(x0, corr)]


def _causal_dwconv(head_ref, b, x, w, tail_ref):
    width = w.shape[0]
    n = x.shape[0]
    xp = jnp.concatenate([head_ref[b], x], axis=0)
    acc = pltpu.roll(xp, width - 1, axis=0)[SUBLANES:] * w[0:1]
    for i in range(1, width - 1):
        acc = acc + pltpu.roll(xp, width - 1 - i, axis=0)[SUBLANES:] * w[i:i + 1]
    acc = acc + x * w[width - 1:width]
    head_ref[b] = xp[n:]
    tail_ref[b] = xp[n + SUBLANES - (width - 1):]
    return acc


def _mixer_kernel(x_ref, cprev_ref, s0_ref, scprev_ref, gpre_ref, wqkvz_ref, wsc_ref, wba_ref, cw_ref, alog_ref,
                  dtb_ref, gnw_ref, scw_ref, wo_ref, gpost_ref,
                  y_ref, cnew_ref, snew_ref, scnew_ref,
                  s_scr, chead, schead, q_scr, k_scr, v_scr, g_scr, beta_scr, o_scr,
                  gc_scr, u_scr, wq_scr, qkk_scr, zg_scr, sc_scr,
                  *, nb, tile, chunk):
    rows = nb * tile

    @pl.when(pl.program_id(1) == 0)
    def _():
        s_scr[...] = s0_ref[...]
        chead[...] = jnp.zeros(chead.shape, F32)
        schead[...] = jnp.zeros(schead.shape, F32)
        for b in range(nb):
            chead[b, SUBLANES - (GDN_CONV - 1):, :] = cprev_ref[b]
            schead[b, SUBLANES - (SC_CONV - 1):, :] = scprev_ref[b]

    x = x_ref[...].reshape(rows, D_MODEL)
    h = _rms(x, gpre_ref[...]).astype(BF16)

    qkv_in = _mm(h, wqkvz_ref[:, :QKV_W])
    pba = _mm(h, wba_ref[...])
    z_in = _mm(h, wqkvz_ref[:, QKV_W:])
    sc_c = _mm(h, wsc_ref[:, SC_W:2 * SC_W])
    sc_h = _mm(h, wsc_ref[:, 2 * SC_W:])
    sc_b = _mm(h, wsc_ref[:, :SC_W])

    cw = cw_ref[...]
    for b in range(nb):
        r0 = b * tile
        acc = _causal_dwconv(chead, b, qkv_in[r0:r0 + tile], cw, cnew_ref)
        act = _silu(acc)
        for hh in range(HEADS):
            lo = hh * HEAD_DIM
            qh = act[:, lo:lo + HEAD_DIM]
            kh = act[:, GDN_W + lo:GDN_W + lo + HEAD_DIM]
            q_scr[r0:r0 + tile, lo:lo + HEAD_DIM] = (
                qh * lax.rsqrt(jnp.sum(qh * qh, axis=-1, keepdims=True) + EPS) * (HEAD_DIM ** -0.5))
            k_scr[r0:r0 + tile, lo:lo + HEAD_DIM] = (
                kh * lax.rsqrt(jnp.sum(kh * kh, axis=-1, keepdims=True) + EPS))
        v_scr[r0:r0 + tile, :] = act[:, 2 * GDN_W:]

    beta_scr[...] = jax.nn.sigmoid(pba)
    ga = pba + dtb_ref[...]
    softplus = jnp.maximum(ga, 0.0) + jnp.log1p(jnp.exp(-jnp.abs(ga)))
    g_scr[...] = -jnp.exp(alog_ref[...]) * softplus

    row = lax.broadcasted_iota(jnp.int32, (chunk, chunk), 0)
    col = lax.broadcasted_iota(jnp.int32, (chunk, chunk), 1)
    causal = row >= col
    strict = row > col
    tri = causal.astype(BF16)

    zg_scr[...] = _silu(z_in)
    cm = sc_c * sc_h
    scw = scw_ref[...]
    for b in range(nb):
        r0 = b * tile
        sc_scr[r0:r0 + tile, :] = sc_b[r0:r0 + tile] * _causal_dwconv(schead, b, cm[r0:r0 + tile], scw, scnew_ref)

    n_chunks = tile // chunk
    cpi = min(PREP_CHUNKS, n_chunks)
    units = [(b, k) for b in range(nb) for k in range(cpi)]
    prep_probs = [(ui, hh) for ui in range(len(units)) for hh in range(HEADS)]
    chain_probs = [(b, hh) for b in range(nb) for hh in range(HEADS)]

    def prep_step(it, carry):
        cidx = [it * cpi + k for _, k in units]
        r0s = [_aligned(b * tile + ci * chunk, chunk) for (b, _), ci in zip(units, cidx)]
        g3 = [_split3(g_scr[pl.ds(r0, chunk), :]) for r0 in r0s]
        gcs = [_mm(tri, t[0]) + (_mm(tri, t[1]) + _mm(tri, t[2])) for t in g3]
        gcts = [gc.T for gc in gcs]
        betas = [beta_scr[pl.ds(r0, chunk), :] for r0 in r0s]
        for r0, gc in zip(r0s, gcs):
            gc_scr[pl.ds(r0, chunk), :] = gc
        ops = []
        for ui, hh in prep_probs:
            lo = hh * HEAD_DIM
            gcol = gcs[ui][:, A_LANE0 + hh:A_LANE0 + hh + 1]
            grow = gcts[ui][A_LANE0 + hh:A_LANE0 + hh + 1, :]
            bcol = betas[ui][:, hh:hh + 1]
            qh = q_scr[pl.ds(r0s[ui], chunk), lo:lo + HEAD_DIM]
            kh = k_scr[pl.ds(r0s[ui], chunk), lo:lo + HEAD_DIM]
            vh = v_scr[pl.ds(r0s[ui], chunk), lo:lo + HEAD_DIM]
            kb = kh * bcol
            eg = jnp.exp(gcol)
            k_t = kh.T
            ops.append(dict(
                decay=jnp.exp(jnp.where(causal, gcol - grow, -jnp.inf)),
                kbq16=jnp.concatenate([kb, qh], axis=0).astype(BF16), kt16=k_t.astype(BF16),
                rhs=jnp.concatenate([vh * bcol, kb * eg], axis=1),
                qg=qh * eg,
                k_tail_t=k_t * jnp.exp(gcol[chunk - 1:chunk, :] - grow)))
        kq = [_mm(o["kbq16"], o["kt16"]) for o in ops]
        a_mats = [jnp.where(strict, p[:chunk] * o["decay"], 0.0) for p, o in zip(kq, ops)]
        qks = [p[chunk:] * o["decay"] for p, o in zip(kq, ops)]
        sols = _solve_refined(a_mats, _unit_lower_inverses(a_mats, row, col, chunk), [o["rhs"] for o in ops])
        for (ui, hh), o, qk, sol in zip(prep_probs, ops, qks, sols):
            lo = hh * HEAD_DIM
            r0 = r0s[ui]
            u_scr[pl.ds(r0, chunk), lo:lo + HEAD_DIM] = sol[:, :HEAD_DIM]
            wq_scr[pl.ds(2 * r0, 2 * chunk), lo:lo + HEAD_DIM] = jnp.concatenate(
                [sol[:, HEAD_DIM:], o["qg"]], axis=0).astype(BF16)
            qkk_scr[(units[ui][0] * n_chunks + cidx[ui]) * HEADS + hh] = jnp.concatenate(
                [qk, o["k_tail_t"]], axis=0).astype(BF16)
        return carry

    def chain_step(c, carry):
        r0s = [_aligned(b * tile + c * chunk, chunk) for b in range(nb)]
        s_old = [s_scr[b, hh] for b, hh in chain_probs]
        ws = [_mm(wq_scr[pl.ds(2 * r0s[b], 2 * chunk), hh * HEAD_DIM:(hh + 1) * HEAD_DIM], s.astype(BF16))
              for (b, hh), s in zip(chain_probs, s_old)]
        v_new = [(u_scr[pl.ds(r0s[b], chunk), hh * HEAD_DIM:(hh + 1) * HEAD_DIM] - w[:chunk]).astype(BF16)
                 for (b, hh), w in zip(chain_probs, ws)]
        r2 = [_mm(qkk_scr[(b * n_chunks + c) * HEADS + hh], v) for (b, hh), v in zip(chain_probs, v_new)]
        for (b, hh), s, w, r in zip(chain_probs, s_old, ws, r2):
            lo = hh * HEAD_DIM
            o_scr[pl.ds(r0s[b], chunk), lo:lo + HEAD_DIM] = w[chunk:] + r[:chunk]
            glast = gc_scr[pl.ds(r0s[b] + (chunk - 1), 1), :][:, A_LANE0 + hh:A_LANE0 + hh + 1]
            s_scr[b, hh] = s * jnp.exp(glast) + r[chunk:]
        return carry

    _loop(n_chunks // cpi, prep_step)
    _loop(n_chunks, chain_step)
    snew_ref[...] = s_scr[...]

    gnw = gnw_ref[...]
    og = []
    for hh in range(HEADS):
        lo = hh * HEAD_DIM
        og.append(_rms(o_scr[:, lo:lo + HEAD_DIM], gnw) * zg_scr[:, lo:lo + HEAD_DIM])
    og = jnp.concatenate(og, axis=1).astype(BF16)
    mix = _mm(og, wo_ref[:GDN_W, :]) + _mm(sc_scr[...].astype(BF16), wo_ref[GDN_W:, :])
    y_ref[...] = (x_ref[...].reshape(rows, D_MODEL) + _rms(mix, gpost_ref[...])).reshape(nb, tile, D_MODEL)


def _ffn_kernel(x_ref, gpre_ref, wg_ref, wu_ref, wd_ref, gpost_ref, y_ref):
    x = x_ref[...]
    h = _rms(x, gpre_ref[...]).astype(BF16)
    f = None
    lo = 0
    for width in FFN_SLABS:
        act = (_silu(_mm(h, wg_ref[:, lo:lo + width])) * _mm(h, wu_ref[:, lo:lo + width])).astype(BF16)
        part = _mm(act, wd_ref[lo:lo + width, :])
        f = part if f is None else f + part
        lo += width
    y_ref[...] = x + _rms(f, gpost_ref[...])


def _const_spec(shape):
    return pl.BlockSpec(shape, lambda *_: (0,) * len(shape))


def _layer_spec(layer, shape):
    return pl.BlockSpec((None,) + shape, lambda *_: (layer, 0, 0))


def _mixer(x, conv_prev, s0, sc_prev, p, stacks, layer, *, nb, tile, chunk):
    batch, seq, _ = x.shape
    rows = nb * tile
    grid = (batch // nb, seq // tile)
    per_b = lambda i, j: (i, 0, 0)
    in_specs = [
        pl.BlockSpec((nb, tile, D_MODEL), lambda i, j: (i, j, 0)),
        pl.BlockSpec((None, nb, GDN_CONV - 1, QKV_W), lambda i, j: (layer, i, 0, 0)),
        pl.BlockSpec((None, nb, HEADS, HEAD_DIM, HEAD_DIM), lambda i, j: (layer, i, 0, 0, 0)),
        pl.BlockSpec((None, nb, SC_CONV - 1, SC_W), lambda i, j: (layer, i, 0, 0)),
        _const_spec((1, D_MODEL)),
        _layer_spec(layer, (D_MODEL, QKVZ_W)),
        _layer_spec(layer, (D_MODEL, 3 * SC_W)),
        _layer_spec(layer, (D_MODEL, LANES)),
        _const_spec((GDN_CONV, QKV_W)),
        _const_spec((1, LANES)),
        _const_spec((1, LANES)),
        _const_spec((1, HEAD_DIM)),
        _const_spec((SC_CONV, SC_W)),
        _layer_spec(layer, (D_MODEL, D_MODEL)),
        _const_spec((1, D_MODEL)),
    ]
    out_specs = [
        pl.BlockSpec((nb, tile, D_MODEL), lambda i, j: (i, j, 0)),
        pl.BlockSpec((nb, GDN_CONV - 1, QKV_W), per_b),
        pl.BlockSpec((nb, HEADS, HEAD_DIM, HEAD_DIM), lambda i, j: (i, 0, 0, 0)),
        pl.BlockSpec((nb, SC_CONV - 1, SC_W), per_b),
    ]
    out_shape = [
        jax.ShapeDtypeStruct(x.shape, F32),
        jax.ShapeDtypeStruct(conv_prev.shape[1:], F32),
        jax.ShapeDtypeStruct(s0.shape[1:], F32),
        jax.ShapeDtypeStruct(sc_prev.shape[1:], F32),
    ]
    scratch = [
        pltpu.VMEM((nb, HEADS, HEAD_DIM, HEAD_DIM), F32),
        pltpu.VMEM((nb, SUBLANES, QKV_W), F32),
        pltpu.VMEM((nb, SUBLANES, SC_W), F32),
        pltpu.VMEM((rows, GDN_W), F32),
        pltpu.VMEM((rows, GDN_W), F32),
        pltpu.VMEM((rows, GDN_W), F32),
        pltpu.VMEM((rows, LANES), F32),
        pltpu.VMEM((rows, LANES), F32),
        pltpu.VMEM((rows, GDN_W), F32),
        pltpu.VMEM((rows, LANES), F32),
        pltpu.VMEM((rows, GDN_W), F32),
        pltpu.VMEM((2 * rows, GDN_W), BF16),
        pltpu.VMEM((rows // chunk * HEADS, chunk + HEAD_DIM, chunk), BF16),
        pltpu.VMEM((rows, GDN_W), F32),
        pltpu.VMEM((rows, SC_W), F32),
    ]
    return pl.pallas_call(
        functools.partial(_mixer_kernel, nb=nb, tile=tile, chunk=chunk),
        grid=grid,
        in_specs=in_specs,
        out_specs=out_specs,
        out_shape=out_shape,
        scratch_shapes=scratch,
        compiler_params=pltpu.CompilerParams(
            dimension_semantics=("arbitrary", "arbitrary"),
            vmem_limit_bytes=VMEM_LIMIT_BYTES),
        name=f"mixer_t{tile}_c{chunk}",
    )(x, conv_prev, s0, sc_prev, p["norm_mix_pre"], stacks["w_in"], stacks["w_sc"], stacks["w_ba"],
      p["conv_qkv_w"], p["a_log"],
      p["dt_bias"], p["gdn_norm_w"], p["conv_sc_w"], stacks["w_o"], p["norm_mix_post"])


def _ffn(x2d, p, stacks, layer, *, tile):
    rows = x2d.shape[0]
    row_spec = pl.BlockSpec((tile, D_MODEL), lambda i: (i, 0))
    return pl.pallas_call(
        _ffn_kernel,
        grid=(rows // tile,),
        in_specs=[
            row_spec,
            _const_spec((1, D_MODEL)),
            _layer_spec(layer, (D_MODEL, D_FF)),
            _layer_spec(layer, (D_MODEL, D_FF)),
            _layer_spec(layer, (D_FF, D_MODEL)),
            _const_spec((1, D_MODEL)),
        ],
        out_specs=row_spec,
        out_shape=jax.ShapeDtypeStruct(x2d.shape, F32),
        compiler_params=pltpu.CompilerParams(
            dimension_semantics=("arbitrary",),
            vmem_limit_bytes=VMEM_LIMIT_BYTES),
        name=f"ffn_t{tile}",
    )(x2d, p["norm_ffn_pre"], stacks["w_gate"], stacks["w_up"], stacks["w_down"], p["norm_ffn_post"])


def _weight_stacks(w_in, w_o, w_gate, w_up, w_down):
    n_ba = 2 * HEADS
    w_ba = jnp.pad(w_in[..., QKVZ_W:QKVZ_W + n_ba].astype(BF16), ((0, 0), (0, 0), (0, LANES - n_ba)))
    return {"w_in": w_in[..., :QKVZ_W].astype(BF16), "w_sc": w_in[..., QKVZ_W + n_ba:].astype(BF16),
            "w_ba": w_ba, "w_o": w_o.astype(BF16),
            "w_gate": w_gate.astype(BF16), "w_up": w_up.astype(BF16), "w_down": w_down.astype(BF16)}


def _layer_vectors(l, norm_mix_pre, conv_qkv_w, a_log, dt_bias, gdn_norm_w, conv_sc_w, norm_mix_post,
                   norm_ffn_pre, norm_ffn_post):
    lane_row = lambda v: jnp.zeros((1, LANES), F32).at[0, A_LANE0:A_LANE0 + HEADS].set(v.astype(F32))
    return {
        "norm_mix_pre": norm_mix_pre[l].reshape(1, D_MODEL),
        "conv_qkv_w": conv_qkv_w[l],
        "a_log": lane_row(a_log[l]),
        "dt_bias": lane_row(dt_bias[l]),
        "gdn_norm_w": gdn_norm_w[l].reshape(1, HEAD_DIM),
        "conv_sc_w": conv_sc_w[l],
        "norm_mix_post": norm_mix_post[l].reshape(1, D_MODEL),
        "norm_ffn_pre": norm_ffn_pre[l].reshape(1, D_MODEL),
        "norm_ffn_post": norm_ffn_post[l].reshape(1, D_MODEL),
    }


def _mixer_tiling(batch, seq):
    chunk = min(CHUNK, seq)
    if seq <= CHUNK:
        return batch, seq, chunk
    return batch, 512, chunk


def _run(x, conv0, s0, sc0, params, stacks):
    batch, seq, _ = x.shape
    nb, tile, chunk = _mixer_tiling(batch, seq)
    ffn_tile = min(FFN_TILE, batch * seq)
    convs, states, scs = [], [], []
    for l, p in enumerate(params):
        x, c, s, sc = _mixer(x, conv0, s0, sc0, p, stacks, l, nb=nb, tile=tile, chunk=chunk)
        x = _ffn(x.reshape(batch * seq, D_MODEL), p, stacks, l, tile=ffn_tile).reshape(batch, seq, D_MODEL)
        convs.append(c)
        states.append(s)
        scs.append(sc)
    return x, jnp.stack(convs), jnp.stack(states), jnp.stack(scs)


def kernel(x_prompt, x_sample, cache_gdn_conv, state_gdn, cache_sc_conv, norm_mix_pre, w_in, conv_qkv_w, a_log, dt_bias, gdn_norm_w, conv_sc_w, w_o, norm_mix_post, norm_ffn_pre, w_gate, w_up, w_down, norm_ffn_post):
    depth = w_in.shape[0]
    stacks = _weight_stacks(w_in, w_o, w_gate, w_up, w_down)
    params = [_layer_vectors(l, norm_mix_pre, conv_qkv_w, a_log, dt_bias, gdn_norm_w, conv_sc_w,
                             norm_mix_post, norm_ffn_pre, norm_ffn_post) for l in range(depth)]
    bp = x_prompt.shape[0]
    zc = jnp.zeros((depth, bp, GDN_CONV - 1, QKV_W), F32)
    zs = jnp.zeros((depth, bp, HEADS, HEAD_DIM, HEAD_DIM), F32)
    zsc = jnp.zeros((depth, bp, SC_CONV - 1, SC_W), F32)
    y_prompt, conv_p, state_p, sc_p = _run(x_prompt, zc, zs, zsc, params, stacks)
    y_sample, conv_s, state_s, sc_s = _run(x_sample, cache_gdn_conv, state_gdn, cache_sc_conv, params, stacks)
    return (y_prompt, y_sample, conv_p, state_p, sc_p, conv_s, state_s, sc_s)
```

```python
import functools

import jax
import jax.numpy as jnp
from jax import lax
from jax.experimental import pallas as pl
from jax.experimental.pallas import tpu as pltpu

F32 = jnp.float32
BF16 = jnp.bfloat16

D_MODEL = 1024
HEADS = 4
HEAD_DIM = 128
GDN_W = HEADS * HEAD_DIM
SC_W = D_MODEL - GDN_W
QKV_W = 3 * GDN_W
GDN_CONV = 4
SC_CONV = 3
D_FF = 2816
CHUNK = 64
EPS = 1e-6

QKVZ_W = QKV_W + GDN_W
GATE_COLS = 2 * HEADS
W_IN_CAST_ROWS = 256
LANES = 128
SUBLANES = 8
A_LANE0 = HEADS

VMEM_LIMIT_BYTES = 56 * 1024 * 1024
MXU_TILE = 256
FFN_SLABS = (3 * MXU_TILE,) * 3 + (2 * MXU_TILE,)
assert sum(FFN_SLABS) == D_FF
FFN_TILE = 1024
PREP_CHUNKS = 4
MAX_UNROLLED_TRIPS = 8


def _rms(x, g):
    return x * lax.rsqrt(jnp.mean(x * x, axis=-1, keepdims=True) + EPS) * g


def _silu(x):
    return x * jax.nn.sigmoid(x)


def _mm(a, b):
    return jnp.dot(a, b, preferred_element_type=F32)


def _aligned(index, multiple):
    return index if isinstance(index, int) else pl.multiple_of(index, multiple)


def _loop(trips, body):
    if trips <= MAX_UNROLLED_TRIPS:
        for i in range(trips):
            body(i, 0)
    else:
        lax.fori_loop(0, trips, body, 0)


def _split2(x):
    hi = x.astype(BF16)
    lo = (x - hi.astype(F32)).astype(BF16)
    return hi, lo


def _split3(x):
    h1 = x.astype(BF16)
    r1 = x - h1.astype(F32)
    h2 = r1.astype(BF16)
    h3 = (r1 - h2.astype(F32)).astype(BF16)
    return h1, h2, h3


def _unit_lower_inverses(a_strict, row, col, size):
    eye = (row == col).astype(F32)
    ts = [eye - jnp.where((row >> 1) == (col >> 1), a, 0.0) for a in a_strict]
    a16 = [a.astype(BF16) for a in a_strict]
    shift = 1
    while (1 << shift) < size:
        off = ((row >> (shift + 1)) == (col >> (shift + 1))) & ((row >> shift) != (col >> shift))
        t16 = [t.astype(BF16) for t in ts]
        inner = [_mm(jnp.where(off, a, jnp.zeros_like(a)), t) for a, t in zip(a16, t16)]
        outer = [_mm(t, i.astype(BF16)) for t, i in zip(t16, inner)]
        ts = [t - o for t, o in zip(ts, outer)]
        shift += 1
    return ts


def _solve_refined(a_strict, t_approx, rhs):
    t16 = [t.astype(BF16) for t in t_approx]
    x0 = [_mm(t, r.astype(BF16)).astype(BF16) for t, r in zip(t16, rhs)]
    a_parts = [_split2(a) for a in a_strict]
    ax_hi = [_mm(a[0], x) for a, x in zip(a_parts, x0)]
    ax_lo = [_mm(a[1], x) for a, x in zip(a_parts, x0)]
    res = [r - x.astype(F32) - (h + l) for r, x, h, l in zip(rhs, x0, ax_hi, ax_lo)]
    corr = [_mm(t, r.astype(BF16)) for t, r in zip(t16, res)]
    return [x.astype(F32) + c for x, c in zip(x0, corr)]


def _causal_dwconv(head_ref, b, x, w, tail_ref):
    width = w.shape[0]
    n = x.shape[0]
    xp = jnp.concatenate([head_ref[b], x], axis=0)
    acc = pltpu.roll(xp, width - 1, axis=0)[SUBLANES:] * w[0:1]
    for i in range(1, width - 1):
        acc = acc + pltpu.roll(xp, width - 1 - i, axis=0)[SUBLANES:] * w[i:i + 1]
    acc = acc + x * w[width - 1:width]
    head_ref[b] = xp[n:]
    tail_ref[b] = xp[n + SUBLANES - (width - 1):]
    return acc


def _mixer_kernel(x_ref, cprev_ref, s0_ref, scprev_ref, gpre_ref, wqkvz_ref, wsc_ref, wba_ref, cw_ref, alog_ref,
                  dtb_ref, gnw_ref, scw_ref, wo_ref, gpost_ref,
                  y_ref, cnew_ref, snew_ref, scnew_ref,
                  s_scr, chead, schead, q_scr, k_scr, v_scr, g_scr, beta_scr, o_scr,
                  gc_scr, u_scr, wq_scr, qkk_scr, zg_scr, sc_scr,
                  *, nb, tile, chunk):
    rows = nb * tile

    @pl.when(pl.program_id(1) == 0)
    def _():
        s_scr[...] = s0_ref[...]
        chead[...] = jnp.zeros(chead.shape, F32)
        schead[...] = jnp.zeros(schead.shape, F32)
        for b in range(nb):
            chead[b, SUBLANES - (GDN_CONV - 1):, :] = cprev_ref[b]
            schead[b, SUBLANES - (SC_CONV - 1):, :] = scprev_ref[b]

    x = x_ref[...].reshape(rows, D_MODEL)
    h = _rms(x, gpre_ref[...]).astype(BF16)

    qkv_in = _mm(h, wqkvz_ref[:, :QKV_W])
    pba = _mm(h, wba_ref[...])
    z_in = _mm(h, wqkvz_ref[:, QKV_W:])
    sc_c = _mm(h, wsc_ref[:, SC_W:2 * SC_W])
    sc_h = _mm(h, wsc_ref[:, 2 * SC_W:])
    sc_b = _mm(h, wsc_ref[:, :SC_W])

    cw = cw_ref[...]
    for b in range(nb):
        r0 = b * tile
        acc = _causal_dwconv(chead, b, qkv_in[r0:r0 + tile], cw, cnew_ref)
        act = _silu(acc)
        for hh in range(HEADS):
            lo = hh * HEAD_DIM
            qh = act[:, lo:lo + HEAD_DIM]
            kh = act[:, GDN_W + lo:GDN_W + lo + HEAD_DIM]
            q_scr[r0:r0 + tile, lo:lo + HEAD_DIM] = (
                qh * lax.rsqrt(jnp.sum(qh * qh, axis=-1, keepdims=True) + EPS) * (HEAD_DIM ** -0.5))
            k_scr[r0:r0 + tile, lo:lo + HEAD_DIM] = (
                kh * lax.rsqrt(jnp.sum(kh * kh, axis=-1, keepdims=True) + EPS))
        v_scr[r0:r0 + tile, :] = act[:, 2 * GDN_W:]

    beta_scr[...] = jax.nn.sigmoid(pba)
    ga = pba + dtb_ref[...]
    softplus = jnp.maximum(ga, 0.0) + jnp.log1p(jnp.exp(-jnp.abs(ga)))
    g_scr[...] = -jnp.exp(alog_ref[...]) * softplus

    row = lax.broadcasted_iota(jnp.int32, (chunk, chunk), 0)
    col = lax.broadcasted_iota(jnp.int32, (chunk, chunk), 1)
    causal = row >= col
    strict = row > col
    tri = causal.astype(BF16)

    zg_scr[...] = _silu(z_in)
    cm = sc_c * sc_h
    scw = scw_ref[...]
    for b in range(nb):
        r0 = b * tile
        sc_scr[r0:r0 + tile, :] = sc_b[r0:r0 + tile] * _causal_dwconv(schead, b, cm[r0:r0 + tile], scw, scnew_ref)

    n_chunks = tile // chunk
    cpi = min(PREP_CHUNKS, n_chunks)
    units = [(b, k) for b in range(nb) for k in range(cpi)]
    prep_probs = [(ui, hh) for ui in range(len(units)) for hh in range(HEADS)]
    chain_probs = [(b, hh) for b in range(nb) for hh in range(HEADS)]

    def prep_step(it, carry):
        cidx = [it * cpi + k for _, k in units]
        r0s = [_aligned(b * tile + ci * chunk, chunk) for (b, _), ci in zip(units, cidx)]
        g3 = [_split3(g_scr[pl.ds(r0, chunk), :]) for r0 in r0s]
        gcs = [_mm(tri, t[0]) + (_mm(tri, t[1]) + _mm(tri, t[2])) for t in g3]
        gcts = [gc.T for gc in gcs]
        betas = [beta_scr[pl.ds(r0, chunk), :] for r0 in r0s]
        for r0, gc in zip(r0s, gcs):
            gc_scr[pl.ds(r0, chunk), :] = gc
        ops = []
        for ui, hh in prep_probs:
            lo = hh * HEAD_DIM
            gcol = gcs[ui][:, A_LANE0 + hh:A_LANE0 + hh + 1]
            grow = gcts[ui][A_LANE0 + hh:A_LANE0 + hh + 1, :]
            bcol = betas[ui][:, hh:hh + 1]
            qh = q_scr[pl.ds(r0s[ui], chunk), lo:lo + HEAD_DIM]
            kh = k_scr[pl.ds(r0s[ui], chunk), lo:lo + HEAD_DIM]
            vh = v_scr[pl.ds(r0s[ui], chunk), lo:lo + HEAD_DIM]
            kb = kh * bcol
            eg = jnp.exp(gcol)
            k_t = kh.T
            ops.append(dict(
                decay=jnp.exp(jnp.where(causal, gcol - grow, -jnp.inf)),
                kbq16=jnp.concatenate([kb, qh], axis=0).astype(BF16), kt16=k_t.astype(BF16),
                rhs=jnp.concatenate([vh * bcol, kb * eg], axis=1),
                qg=qh * eg,
                k_tail_t=k_t * jnp.exp(gcol[chunk - 1:chunk, :] - grow)))
        kq = [_mm(o["kbq16"], o["kt16"]) for o in ops]
        a_mats = [jnp.where(strict, p[:chunk] * o["decay"], 0.0) for p, o in zip(kq, ops)]
        qks = [p[chunk:] * o["decay"] for p, o in zip(kq, ops)]
        sols = _solve_refined(a_mats, _unit_lower_inverses(a_mats, row, col, chunk), [o["rhs"] for o in ops])
        for (ui, hh), o, qk, sol in zip(prep_probs, ops, qks, sols):
            lo = hh * HEAD_DIM
            r0 = r0s[ui]
            u_scr[pl.ds(r0, chunk), lo:lo + HEAD_DIM] = sol[:, :HEAD_DIM]
            wq_scr[pl.ds(2 * r0, 2 * chunk), lo:lo + HEAD_DIM] = jnp.concatenate(
                [sol[:, HEAD_DIM:], o["qg"]], axis=0).astype(BF16)
            qkk_scr[(units[ui][0] * n_chunks + cidx[ui]) * HEADS + hh] = jnp.concatenate(
                [qk, o["k_tail_t"]], axis=0).astype(BF16)
        return carry

    def chain_step(c, carry):
        r0s = [_aligned(b * tile + c * chunk, chunk) for b in range(nb)]
        s_old = [s_scr[b, hh] for b, hh in chain_probs]
        ws = [_mm(wq_scr[pl.ds(2 * r0s[b], 2 * chunk), hh * HEAD_DIM:(hh + 1) * HEAD_DIM], s.astype(BF16))
              for (b, hh), s in zip(chain_probs, s_old)]
        v_new = [(u_scr[pl.ds(r0s[b], chunk), hh * HEAD_DIM:(hh + 1) * HEAD_DIM] - w[:chunk]).astype(BF16)
                 for (b, hh), w in zip(chain_probs, ws)]
        r2 = [_mm(qkk_scr[(b * n_chunks + c) * HEADS + hh], v) for (b, hh), v in zip(chain_probs, v_new)]
        for (b, hh), s, w, r in zip(chain_probs, s_old, ws, r2):
            lo = hh * HEAD_DIM
            o_scr[pl.ds(r0s[b], chunk), lo:lo + HEAD_DIM] = w[chunk:] + r[:chunk]
            glast = gc_scr[pl.ds(r0s[b] + (chunk - 1), 1), :][:, A_LANE0 + hh:A_LANE0 + hh + 1]
            s_scr[b, hh] = s * jnp.exp(glast) + r[chunk:]
        return carry

    _loop(n_chunks // cpi, prep_step)
    _loop(n_chunks, chain_step)
    snew_ref[...] = s_scr[...]

    gnw = gnw_ref[...]
    og = []
    for hh in range(HEADS):
        lo = hh * HEAD_DIM
        og.append(_rms(o_scr[:, lo:lo + HEAD_DIM], gnw) * zg_scr[:, lo:lo + HEAD_DIM])
    og = jnp.concatenate(og, axis=1).astype(BF16)
    mix = _mm(og, wo_ref[:GDN_W, :]) + _mm(sc_scr[...].astype(BF16), wo_ref[GDN_W:, :])
    y_ref[...] = (x_ref[...].reshape(rows, D_MODEL) + _rms(mix, gpost_ref[...])).reshape(nb, tile, D_MODEL)


def _ffn_kernel(x_ref, gpre_ref, wg_ref, wu_ref, wd_ref, gpost_ref, y_ref):
    x = x_ref[...]
    h = _rms(x, gpre_ref[...]).astype(BF16)
    f = None
    lo = 0
    for width in FFN_SLABS:
        act = (_silu(_mm(h, wg_ref[:, lo:lo + width])) * _mm(h, wu_ref[:, lo:lo + width])).astype(BF16)
        part = _mm(act, wd_ref[lo:lo + width, :])
        f = part if f is None else f + part
        lo += width
    y_ref[...] = x + _rms(f, gpost_ref[...])


def _const_spec(shape):
    return pl.BlockSpec(shape, lambda *_: (0,) * len(shape))


def _layer_spec(layer, shape):
    return pl.BlockSpec((None,) + shape, lambda *_: (layer, 0, 0))


def _mixer(x, conv_prev, s0, sc_prev, p, stacks, layer, *, nb, tile, chunk):
    batch, seq, _ = x.shape
    rows = nb * tile
    grid = (batch // nb, seq // tile)
    per_b = lambda i, j: (i, 0, 0)
    in_specs = [
        pl.BlockSpec((nb, tile, D_MODEL), lambda i, j: (i, j, 0)),
        pl.BlockSpec((None, nb, GDN_CONV - 1, QKV_W), lambda i, j: (layer, i, 0, 0)),
        pl.BlockSpec((None, nb, HEADS, HEAD_DIM, HEAD_DIM), lambda i, j: (layer, i, 0, 0, 0)),
        pl.BlockSpec((None, nb, SC_CONV - 1, SC_W), lambda i, j: (layer, i, 0, 0)),
        _const_spec((1, D_MODEL)),
        _layer_spec(layer, (D_MODEL, QKVZ_W)),
        _layer_spec(layer, (D_MODEL, 3 * SC_W)),
        _layer_spec(layer, (D_MODEL, LANES)),
        _const_spec((GDN_CONV, QKV_W)),
        _const_spec((1, LANES)),
        _const_spec((1, LANES)),
        _const_spec((1, HEAD_DIM)),
        _const_spec((SC_CONV, SC_W)),
        _layer_spec(layer, (D_MODEL, D_MODEL)),
        _const_spec((1, D_MODEL)),
    ]
    out_specs = [
        pl.BlockSpec((nb, tile, D_MODEL), lambda i, j: (i, j, 0)),
        pl.BlockSpec((nb, GDN_CONV - 1, QKV_W), per_b),
        pl.BlockSpec((nb, HEADS, HEAD_DIM, HEAD_DIM), lambda i, j: (i, 0, 0, 0)),
        pl.BlockSpec((nb, SC_CONV - 1, SC_W), per_b),
    ]
    out_shape = [
        jax.ShapeDtypeStruct(x.shape, F32),
        jax.ShapeDtypeStruct(conv_prev.shape[1:], F32),
        jax.ShapeDtypeStruct(s0.shape[1:], F32),
        jax.ShapeDtypeStruct(sc_prev.shape[1:], F32),
    ]
    scratch = [
        pltpu.VMEM((nb, HEADS, HEAD_DIM, HEAD_DIM), F32),
        pltpu.VMEM((nb, SUBLANES, QKV_W), F32),
        pltpu.VMEM((nb, SUBLANES, SC_W), F32),
        pltpu.VMEM((rows, GDN_W), F32),
        pltpu.VMEM((rows, GDN_W), F32),
        pltpu.VMEM((rows, GDN_W), F32),
        pltpu.VMEM((rows, LANES), F32),
        pltpu.VMEM((rows, LANES), F32),
        pltpu.VMEM((rows, GDN_W), F32),
        pltpu.VMEM((rows, LANES), F32),
        pltpu.VMEM((rows, GDN_W), F32),
        pltpu.VMEM((2 * rows, GDN_W), BF16),
        pltpu.VMEM((rows // chunk * HEADS, chunk + HEAD_DIM, chunk), BF16),
        pltpu.VMEM((rows, GDN_W), F32),
        pltpu.VMEM((rows, SC_W), F32),
    ]
    return pl.pallas_call(
        functools.partial(_mixer_kernel, nb=nb, tile=tile, chunk=chunk),
        grid=grid,
        in_specs=in_specs,
        out_specs=out_specs,
        out_shape=out_shape,
        scratch_shapes=scratch,
        compiler_params=pltpu.CompilerParams(
            dimension_semantics=("arbitrary", "arbitrary"),
            vmem_limit_bytes=VMEM_LIMIT_BYTES),
        name=f"mixer_t{tile}_c{chunk}",
    )(x, conv_prev, s0, sc_prev, p["norm_mix_pre"], stacks["w_in"], stacks["w_sc"], stacks["w_ba"],
      p["conv_qkv_w"], p["a_log"],
      p["dt_bias"], p["gdn_norm_w"], p["conv_sc_w"], stacks["w_o"], p["norm_mix_post"])


def _ffn(x2d, p, stacks, layer, *, tile):
    rows = x2d.shape[0]
    row_spec = pl.BlockSpec((tile, D_MODEL), lambda i: (i, 0))
    return pl.pallas_call(
        _ffn_kernel,
        grid=(rows // tile,),
        in_specs=[
            row_spec,
            _const_spec((1, D_MODEL)),
            _layer_spec(layer, (D_MODEL, D_FF)),
            _layer_spec(layer, (D_MODEL, D_FF)),
            _layer_spec(layer, (D_FF, D_MODEL)),
            _const_spec((1, D_MODEL)),
        ],
        out_specs=row_spec,
        out_shape=jax.ShapeDtypeStruct(x2d.shape, F32),
        compiler_params=pltpu.CompilerParams(
            dimension_semantics=("arbitrary",),
            vmem_limit_bytes=VMEM_LIMIT_BYTES),
        name=f"ffn_t{tile}",
    )(x2d, p["norm_ffn_pre"], stacks["w_gate"], stacks["w_up"], stacks["w_down"], p["norm_ffn_post"])


def _w_in_groups_kernel(w_ref, qkvz_ref, sc_ref, ba_ref):
    w = w_ref[...]
    qkvz_ref[...] = w[:, :QKVZ_W].astype(BF16)
    sc_ref[...] = w[:, QKVZ_W + GATE_COLS:].astype(BF16)
    gate = w[:, QKVZ_W:QKVZ_W + LANES]
    lane = lax.broadcasted_iota(jnp.int32, gate.shape, 1)
    ba_ref[...] = jnp.where(lane < GATE_COLS, gate, 0.0).astype(BF16)


def _w_in_groups(w_in):
    depth, d_in, cols = w_in.shape
    group = lambda width: pl.BlockSpec((None, W_IN_CAST_ROWS, width), lambda l, r: (l, r, 0))
    return pl.pallas_call(
        _w_in_groups_kernel,
        grid=(depth, d_in // W_IN_CAST_ROWS),
        in_specs=[group(cols)],
        out_specs=[group(QKVZ_W), group(3 * SC_W), group(LANES)],
        out_shape=[jax.ShapeDtypeStruct((depth, d_in, QKVZ_W), BF16),
                   jax.ShapeDtypeStruct((depth, d_in, 3 * SC_W), BF16),
                   jax.ShapeDtypeStruct((depth, d_in, LANES), BF16)],
        name="w_in_groups",
    )(w_in)


def _weight_stacks(w_in, w_o, w_gate, w_up, w_down):
    w_qkvz, w_sc, w_ba = _w_in_groups(w_in)
    return {"w_in": w_qkvz, "w_sc": w_sc, "w_ba": w_ba, "w_o": w_o.astype(BF16),
            "w_gate": w_gate.astype(BF16), "w_up": w_up.astype(BF16), "w_down": w_down.astype(BF16)}


def _layer_vectors(l, norm_mix_pre, conv_qkv_w, a_log, dt_bias, gdn_norm_w, conv_sc_w, norm_mix_post,
                   norm_ffn_pre, norm_ffn_post):
    lane_row = lambda v: jnp.zeros((1, LANES), F32).at[0, A_LANE0:A_LANE0 + HEADS].set(v.astype(F32))
    return {
        "norm_mix_pre": norm_mix_pre[l].reshape(1, D_MODEL),
        "conv_qkv_w": conv_qkv_w[l],
        "a_log": lane_row(a_log[l]),
        "dt_bias": lane_row(dt_bias[l]),
        "gdn_norm_w": gdn_norm_w[l].reshape(1, HEAD_DIM),
        "conv_sc_w": conv_sc_w[l],
        "norm_mix_post": norm_mix_post[l].reshape(1, D_MODEL),
        "norm_ffn_pre": norm_ffn_pre[l].reshape(1, D_MODEL),
        "norm_ffn_post": norm_ffn_post[l].reshape(1, D_MODEL),
    }


def _mixer_tiling(batch, seq):
    chunk = min(CHUNK, seq)
    if seq <= CHUNK:
        return batch, seq, chunk
    return batch, 512, chunk


def _run(x, conv0, s0, sc0, params, stacks):
    batch, seq, _ = x.shape
    nb, tile, chunk = _mixer_tiling(batch, seq)
    ffn_tile = min(FFN_TILE, batch * seq)
    convs, states, scs = [], [], []
    for l, p in enumerate(params):
        x, c, s, sc = _mixer(x, conv0, s0, sc0, p, stacks, l, nb=nb, tile=tile, chunk=chunk)
        x = _ffn(x.reshape(batch * seq, D_MODEL), p, stacks, l, tile=ffn_tile).reshape(batch, seq, D_MODEL)
        convs.append(c)
        states.append(s)
        scs.append(sc)
    return x, jnp.stack(convs), jnp.stack(states), jnp.stack(scs)


def kernel(x_prompt, x_sample, cache_gdn_conv, state_gdn, cache_sc_conv, norm_mix_pre, w_in, conv_qkv_w, a_log, dt_bias, gdn_norm_w, conv_sc_w, w_o, norm_mix_post, norm_ffn_pre, w_gate, w_up, w_down, norm_ffn_post):
    depth = w_in.shape[0]
    stacks = _weight_stacks(w_in, w_o, w_gate, w_up, w_down)
    params = [_layer_vectors(l, norm_mix_pre, conv_qkv_w, a_log, dt_bias, gdn_norm_w, conv_sc_w,
                             norm_mix_post, norm_ffn_pre, norm_ffn_post) for l in range(depth)]
    bp = x_prompt.shape[0]
    zc = jnp.zeros((depth, bp, GDN_CONV - 1, QKV_W), F32)
    zs = jnp.zeros((depth, bp, HEADS, HEAD_DIM, HEAD_DIM), F32)
    zsc = jnp.zeros((depth, bp, SC_CONV - 1, SC_W), F32)
    y_prompt, conv_p, state_p, sc_p = _run(x_prompt, zc, zs, zsc, params, stacks)
    y_sample, conv_s, state_s, sc_s = _run(x_sample, cache_gdn_conv, state_gdn, cache_sc_conv, params, stacks)
    return (y_prompt, y_sample, conv_p, state_p, sc_p, conv_s, state_s, sc_s)
```

```python
import functools

import jax
import jax.numpy as jnp
from jax import lax
from jax.experimental import pallas as pl
from jax.experimental.pallas import tpu as pltpu

F32 = jnp.float32
BF16 = jnp.bfloat16

D_MODEL = 1024
HEADS = 4
HEAD_DIM = 128
GDN_W = HEADS * HEAD_DIM
SC_W = D_MODEL - GDN_W
QKV_W = 3 * GDN_W
GDN_CONV = 4
SC_CONV = 3
D_FF = 2816
CHUNK = 64
EPS = 1e-6

QKVZ_W = QKV_W + GDN_W
GATE_COLS = 2 * HEADS
W_IN_CAST_ROWS = 256
LANES = 128
SUBLANES = 8
A_LANE0 = HEADS

VMEM_LIMIT_BYTES = 56 * 1024 * 1024
MXU_TILE = 256
FFN_SLABS = (3 * MXU_TILE,) * 3 + (2 * MXU_TILE,)
assert sum(FFN_SLABS) == D_FF
FFN_TILE = 1024
MIXER_TILE = 512
PREP_CHUNKS = 4
MAX_UNROLLED_TRIPS = 8


def _rms(x, g):
    return x * lax.rsqrt(jnp.mean(x * x, axis=-1, keepdims=True) + EPS) * g


def _silu(x):
    return x * jax.nn.sigmoid(x)


def _mm(a, b):
    return jnp.dot(a, b, preferred_element_type=F32)


def _aligned(index, multiple):
    return index if isinstance(index, int) else pl.multiple_of(index, multiple)


def _loop(trips, body):
    if trips <= MAX_UNROLLED_TRIPS:
        for i in range(trips):
            body(i, 0)
    else:
        lax.fori_loop(0, trips, body, 0)


def _split2(x):
    hi = x.astype(BF16)
    lo = (x - hi.astype(F32)).astype(BF16)
    return hi, lo


def _split3(x):
    h1 = x.astype(BF16)
    r1 = x - h1.astype(F32)
    h2 = r1.astype(BF16)
    h3 = (r1 - h2.astype(F32)).astype(BF16)
    return h1, h2, h3


def _unit_lower_inverses(a_strict, row, col, size):
    eye = (row == col).astype(F32)
    ts = [eye - jnp.where((row >> 1) == (col >> 1), a, 0.0) for a in a_strict]
    a16 = [a.astype(BF16) for a in a_strict]
    shift = 1
    while (1 << shift) < size:
        off = ((row >> (shift + 1)) == (col >> (shift + 1))) & ((row >> shift) != (col >> shift))
        t16 = [t.astype(BF16) for t in ts]
        inner = [_mm(jnp.where(off, a, jnp.zeros_like(a)), t) for a, t in zip(a16, t16)]
        outer = [_mm(t, i.astype(BF16)) for t, i in zip(t16, inner)]
        ts = [t - o for t, o in zip(ts, outer)]
        shift += 1
    return ts


def _solve_refined(a_strict, t_approx, rhs):
    t16 = [t.astype(BF16) for t in t_approx]
    x0 = [_mm(t, r.astype(BF16)).astype(BF16) for t, r in zip(t16, rhs)]
    a_parts = [_split2(a) for a in a_strict]
    ax_hi = [_mm(a[0], x) for a, x in zip(a_parts, x0)]
    ax_lo = [_mm(a[1], x) for a, x in zip(a_parts, x0)]
    res = [r - x.astype(F32) - (h + l) for r, x, h, l in zip(rhs, x0, ax_hi, ax_lo)]
    corr = [_mm(t, r.astype(BF16)) for t, r in zip(t16, res)]
    return [x.astype(F32) + c for x, c in zip(x0, corr)]


def _causal_dwconv(head_ref, b, x, w, tail_ref):
    width = w.shape[0]
    n = x.shape[0]
    xp = jnp.concatenate([head_ref[b], x], axis=0)
    acc = pltpu.roll(xp, width - 1, axis=0)[SUBLANES:] * w[0:1]
    for i in range(1, width - 1):
        acc = acc + pltpu.roll(xp, width - 1 - i, axis=0)[SUBLANES:] * w[i:i + 1]
    acc = acc + x * w[width - 1:width]
    head_ref[b] = xp[n:]
    tail_ref[b] = xp[n + SUBLANES - (width - 1):]
    return acc


def _mixer_kernel(x_ref, cprev_ref, s0_ref, scprev_ref, gpre_ref, wqkvz_ref, wsc_ref, wba_ref, cw_ref, alog_ref,
                  dtb_ref, gnw_ref, scw_ref, wo_ref, gpost_ref,
                  y_ref, cnew_ref, snew_ref, scnew_ref,
                  s_scr, chead, schead, q_scr, k_scr, v_scr, g_scr, beta_scr, o_scr,
                  gc_scr, u_scr, wq_scr, qkk_scr, zg_scr, sc_scr,
                  *, nb, tile, chunk):
    rows = nb * tile

    @pl.when(pl.program_id(1) == 0)
    def _():
        s_scr[...] = s0_ref[...]
        chead[...] = jnp.zeros(chead.shape, F32)
        schead[...] = jnp.zeros(schead.shape, F32)
        for b in range(nb):
            chead[b, SUBLANES - (GDN_CONV - 1):, :] = cprev_ref[b]
            schead[b, SUBLANES - (SC_CONV - 1):, :] = scprev_ref[b]

    x = x_ref[...].reshape(rows, D_MODEL)
    h = _rms(x, gpre_ref[...]).astype(BF16)

    qkv_in = _mm(h, wqkvz_ref[:, :QKV_W])
    pba = _mm(h, wba_ref[...])
    z_in = _mm(h, wqkvz_ref[:, QKV_W:])
    sc_c = _mm(h, wsc_ref[:, SC_W:2 * SC_W])
    sc_h = _mm(h, wsc_ref[:, 2 * SC_W:])
    sc_b = _mm(h, wsc_ref[:, :SC_W])

    cw = cw_ref[...]
    for b in range(nb):
        r0 = b * tile
        acc = _causal_dwconv(chead, b, qkv_in[r0:r0 + tile], cw, cnew_ref)
        act = _silu(acc)
        for hh in range(HEADS):
            lo = hh * HEAD_DIM
            qh = act[:, lo:lo + HEAD_DIM]
            kh = act[:, GDN_W + lo:GDN_W + lo + HEAD_DIM]
            q_scr[r0:r0 + tile, lo:lo + HEAD_DIM] = (
                qh * lax.rsqrt(jnp.sum(qh * qh, axis=-1, keepdims=True) + EPS) * (HEAD_DIM ** -0.5))
            k_scr[r0:r0 + tile, lo:lo + HEAD_DIM] = (
                kh * lax.rsqrt(jnp.sum(kh * kh, axis=-1, keepdims=True) + EPS))
        v_scr[r0:r0 + tile, :] = act[:, 2 * GDN_W:]

    beta_scr[...] = jax.nn.sigmoid(pba)
    ga = pba + dtb_ref[...]
    softplus = jnp.maximum(ga, 0.0) + jnp.log1p(jnp.exp(-jnp.abs(ga)))
    g_scr[...] = -jnp.exp(alog_ref[...]) * softplus

    row = lax.broadcasted_iota(jnp.int32, (chunk, chunk), 0)
    col = lax.broadcasted_iota(jnp.int32, (chunk, chunk), 1)
    causal = row >= col
    strict = row > col
    tri = causal.astype(BF16)

    zg_scr[...] = _silu(z_in)
    cm = sc_c * sc_h
    scw = scw_ref[...]
    for b in range(nb):
        r0 = b * tile
        sc_scr[r0:r0 + tile, :] = sc_b[r0:r0 + tile] * _causal_dwconv(schead, b, cm[r0:r0 + tile], scw, scnew_ref)

    n_chunks = tile // chunk
    cpi = min(PREP_CHUNKS, n_chunks)
    units = [(b, k) for b in range(nb) for k in range(cpi)]
    prep_probs = [(ui, hh) for ui in range(len(units)) for hh in range(HEADS)]
    chain_probs = [(b, hh) for b in range(nb) for hh in range(HEADS)]

    def prep_step(it, carry):
        cidx = [it * cpi + k for _, k in units]
        r0s = [_aligned(b * tile + ci * chunk, chunk) for (b, _), ci in zip(units, cidx)]
        g3 = [_split3(g_scr[pl.ds(r0, chunk), :]) for r0 in r0s]
        gcs = [_mm(tri, t[0]) + (_mm(tri, t[1]) + _mm(tri, t[2])) for t in g3]
        gcts = [gc.T for gc in gcs]
        betas = [beta_scr[pl.ds(r0, chunk), :] for r0 in r0s]
        for r0, gc in zip(r0s, gcs):
            gc_scr[pl.ds(r0, chunk), :] = gc
        ops = []
        for ui, hh in prep_probs:
            lo = hh * HEAD_DIM
            gcol = gcs[ui][:, A_LANE0 + hh:A_LANE0 + hh + 1]
            grow = gcts[ui][A_LANE0 + hh:A_LANE0 + hh + 1, :]
            bcol = betas[ui][:, hh:hh + 1]
            qh = q_scr[pl.ds(r0s[ui], chunk), lo:lo + HEAD_DIM]
            kh = k_scr[pl.ds(r0s[ui], chunk), lo:lo + HEAD_DIM]
            vh = v_scr[pl.ds(r0s[ui], chunk), lo:lo + HEAD_DIM]
            kb = kh * bcol
            eg = jnp.exp(gcol)
            k_t = kh.T
            ops.append(dict(
                decay=jnp.exp(jnp.where(causal, gcol - grow, -jnp.inf)),
                kbq16=jnp.concatenate([kb, qh], axis=0).astype(BF16), kt16=k_t.astype(BF16),
                rhs=jnp.concatenate([vh * bcol, kb * eg], axis=1),
                qg=qh * eg,
                k_tail_t=k_t * jnp.exp(gcol[chunk - 1:chunk, :] - grow)))
        kq = [_mm(o["kbq16"], o["kt16"]) for o in ops]
        a_mats = [jnp.where(strict, p[:chunk] * o["decay"], 0.0) for p, o in zip(kq, ops)]
        qks = [p[chunk:] * o["decay"] for p, o in zip(kq, ops)]
        sols = _solve_refined(a_mats, _unit_lower_inverses(a_mats, row, col, chunk), [o["rhs"] for o in ops])
        for (ui, hh), o, qk, sol in zip(prep_probs, ops, qks, sols):
            lo = hh * HEAD_DIM
            r0 = r0s[ui]
            u_scr[pl.ds(r0, chunk), lo:lo + HEAD_DIM] = sol[:, :HEAD_DIM]
            wq_scr[pl.ds(2 * r0, 2 * chunk), lo:lo + HEAD_DIM] = jnp.concatenate(
                [sol[:, HEAD_DIM:], o["qg"]], axis=0).astype(BF16)
            qkk_scr[(units[ui][0] * n_chunks + cidx[ui]) * HEADS + hh] = jnp.concatenate(
                [qk, o["k_tail_t"]], axis=0).astype(BF16)
        return carry

    def chain_step(c, carry):
        r0s = [_aligned(b * tile + c * chunk, chunk) for b in range(nb)]
        s_old = [s_scr[b, hh] for b, hh in chain_probs]
        ws = [_mm(wq_scr[pl.ds(2 * r0s[b], 2 * chunk), hh * HEAD_DIM:(hh + 1) * HEAD_DIM], s.astype(BF16))
              for (b, hh), s in zip(chain_probs, s_old)]
        v_new = [(u_scr[pl.ds(r0s[b], chunk), hh * HEAD_DIM:(hh + 1) * HEAD_DIM] - w[:chunk]).astype(BF16)
                 for (b, hh), w in zip(chain_probs, ws)]
        r2 = [_mm(qkk_scr[(b * n_chunks + c) * HEADS + hh], v) for (b, hh), v in zip(chain_probs, v_new)]
        for (b, hh), s, w, r in zip(chain_probs, s_old, ws, r2):
            lo = hh * HEAD_DIM
            o_scr[pl.ds(r0s[b], chunk), lo:lo + HEAD_DIM] = w[chunk:] + r[:chunk]
            glast = gc_scr[pl.ds(r0s[b] + (chunk - 1), 1), :][:, A_LANE0 + hh:A_LANE0 + hh + 1]
            s_scr[b, hh] = s * jnp.exp(glast) + r[chunk:]
        return carry

    if n_chunks <= MAX_UNROLLED_TRIPS:
        for it in range(n_chunks // cpi):
            prep_step(it, 0)
            for c in range(it * cpi, (it + 1) * cpi):
                chain_step(c, 0)
    else:
        _loop(n_chunks // cpi, prep_step)
        _loop(n_chunks, chain_step)
    snew_ref[...] = s_scr[...]

    gnw = gnw_ref[...]
    og = []
    for hh in range(HEADS):
        lo = hh * HEAD_DIM
        og.append(_rms(o_scr[:, lo:lo + HEAD_DIM], gnw) * zg_scr[:, lo:lo + HEAD_DIM])
    mix = _mm(jnp.concatenate(og + [sc_scr[...]], axis=1).astype(BF16), wo_ref[...])
    y_ref[...] = (x_ref[...].reshape(rows, D_MODEL) + _rms(mix, gpost_ref[...])).reshape(nb, tile, D_MODEL)


def _ffn_kernel(x_ref, gpre_ref, wg_ref, wu_ref, wd_ref, gpost_ref, y_ref):
    x = x_ref[...]
    h = _rms(x, gpre_ref[...]).astype(BF16)
    f = None
    lo = 0
    for width in FFN_SLABS:
        act = (_silu(_mm(h, wg_ref[:, lo:lo + width])) * _mm(h, wu_ref[:, lo:lo + width])).astype(BF16)
        part = _mm(act, wd_ref[lo:lo + width, :])
        f = part if f is None else f + part
        lo += width
    y_ref[...] = x + _rms(f, gpost_ref[...])


def _const_spec(shape):
    return pl.BlockSpec(shape, lambda *_: (0,) * len(shape))


def _layer_spec(layer, shape):
    return pl.BlockSpec((None,) + shape, lambda *_: (layer, 0, 0))


def _mixer(x, conv_prev, s0, sc_prev, p, stacks, layer, *, nb, tile, chunk):
    batch, seq, _ = x.shape
    rows = nb * tile
    grid = (batch // nb, seq // tile)
    per_b = lambda i, j: (i, 0, 0)
    in_specs = [
        pl.BlockSpec((nb, tile, D_MODEL), lambda i, j: (i, j, 0)),
        pl.BlockSpec((None, nb, GDN_CONV - 1, QKV_W), lambda i, j: (layer, i, 0, 0)),
        pl.BlockSpec((None, nb, HEADS, HEAD_DIM, HEAD_DIM), lambda i, j: (layer, i, 0, 0, 0)),
        pl.BlockSpec((None, nb, SC_CONV - 1, SC_W), lambda i, j: (layer, i, 0, 0)),
        _const_spec((1, D_MODEL)),
        _layer_spec(layer, (D_MODEL, QKVZ_W)),
        _layer_spec(layer, (D_MODEL, 3 * SC_W)),
        _layer_spec(layer, (D_MODEL, LANES)),
        _const_spec((GDN_CONV, QKV_W)),
        _const_spec((1, LANES)),
        _const_spec((1, LANES)),
        _const_spec((1, HEAD_DIM)),
        _const_spec((SC_CONV, SC_W)),
        _layer_spec(layer, (D_MODEL, D_MODEL)),
        _const_spec((1, D_MODEL)),
    ]
    out_specs = [
        pl.BlockSpec((nb, tile, D_MODEL), lambda i, j: (i, j, 0)),
        pl.BlockSpec((nb, GDN_CONV - 1, QKV_W), per_b),
        pl.BlockSpec((nb, HEADS, HEAD_DIM, HEAD_DIM), lambda i, j: (i, 0, 0, 0)),
        pl.BlockSpec((nb, SC_CONV - 1, SC_W), per_b),
    ]
    out_shape = [
        jax.ShapeDtypeStruct(x.shape, F32),
        jax.ShapeDtypeStruct(conv_prev.shape[1:], F32),
        jax.ShapeDtypeStruct(s0.shape[1:], F32),
        jax.ShapeDtypeStruct(sc_prev.shape[1:], F32),
    ]
    scratch = [
        pltpu.VMEM((nb, HEADS, HEAD_DIM, HEAD_DIM), F32),
        pltpu.VMEM((nb, SUBLANES, QKV_W), F32),
        pltpu.VMEM((nb, SUBLANES, SC_W), F32),
        pltpu.VMEM((rows, GDN_W), F32),
        pltpu.VMEM((rows, GDN_W), F32),
        pltpu.VMEM((rows, GDN_W), F32),
        pltpu.VMEM((rows, LANES), F32),
        pltpu.VMEM((rows, LANES), F32),
        pltpu.VMEM((rows, GDN_W), F32),
        pltpu.VMEM((rows, LANES), F32),
        pltpu.VMEM((rows, GDN_W), F32),
        pltpu.VMEM((2 * rows, GDN_W), BF16),
        pltpu.VMEM((rows // chunk * HEADS, chunk + HEAD_DIM, chunk), BF16),
        pltpu.VMEM((rows, GDN_W), F32),
        pltpu.VMEM((rows, SC_W), F32),
    ]
    return pl.pallas_call(
        functools.partial(_mixer_kernel, nb=nb, tile=tile, chunk=chunk),
        grid=grid,
        in_specs=in_specs,
        out_specs=out_specs,
        out_shape=out_shape,
        scratch_shapes=scratch,
        compiler_params=pltpu.CompilerParams(
            dimension_semantics=("arbitrary", "arbitrary"),
            vmem_limit_bytes=VMEM_LIMIT_BYTES),
        name=f"mixer_t{tile}_c{chunk}",
    )(x, conv_prev, s0, sc_prev, p["norm_mix_pre"], stacks["w_in"], stacks["w_sc"], stacks["w_ba"],
      p["conv_qkv_w"], p["a_log"],
      p["dt_bias"], p["gdn_norm_w"], p["conv_sc_w"], stacks["w_o"], p["norm_mix_post"])


def _ffn(x2d, p, stacks, layer, *, tile):
    rows = x2d.shape[0]
    row_spec = pl.BlockSpec((tile, D_MODEL), lambda i: (i, 0))
    return pl.pallas_call(
        _ffn_kernel,
        grid=(rows // tile,),
        in_specs=[
            row_spec,
            _const_spec((1, D_MODEL)),
            _layer_spec(layer, (D_MODEL, D_FF)),
            _layer_spec(layer, (D_MODEL, D_FF)),
            _layer_spec(layer, (D_FF, D_MODEL)),
            _const_spec((1, D_MODEL)),
        ],
        out_specs=row_spec,
        out_shape=jax.ShapeDtypeStruct(x2d.shape, F32),
        compiler_params=pltpu.CompilerParams(
            dimension_semantics=("arbitrary",),
            vmem_limit_bytes=VMEM_LIMIT_BYTES),
        name=f"ffn_t{tile}",
    )(x2d, p["norm_ffn_pre"], stacks["w_gate"], stacks["w_up"], stacks["w_down"], p["norm_ffn_post"])


def _w_in_groups_kernel(w_ref, qkvz_ref, sc_ref, ba_ref):
    w = w_ref[...]
    qkvz_ref[...] = w[:, :QKVZ_W].astype(BF16)
    sc_ref[...] = w[:, QKVZ_W + GATE_COLS:].astype(BF16)
    gate = w[:, QKVZ_W:QKVZ_W + LANES]
    lane = lax.broadcasted_iota(jnp.int32, gate.shape, 1)
    ba_ref[...] = jnp.where(lane < GATE_COLS, gate, 0.0).astype(BF16)


def _w_in_groups(w_in):
    depth, d_in, cols = w_in.shape
    group = lambda width: pl.BlockSpec((None, W_IN_CAST_ROWS, width), lambda l, r: (l, r, 0))
    return pl.pallas_call(
        _w_in_groups_kernel,
        grid=(depth, d_in // W_IN_CAST_ROWS),
        in_specs=[group(cols)],
        out_specs=[group(QKVZ_W), group(3 * SC_W), group(LANES)],
        out_shape=[jax.ShapeDtypeStruct((depth, d_in, QKVZ_W), BF16),
                   jax.ShapeDtypeStruct((depth, d_in, 3 * SC_W), BF16),
                   jax.ShapeDtypeStruct((depth, d_in, LANES), BF16)],
        name="w_in_groups",
    )(w_in)


def _weight_stacks(w_in, w_o, w_gate, w_up, w_down):
    w_qkvz, w_sc, w_ba = _w_in_groups(w_in)
    return {"w_in": w_qkvz, "w_sc": w_sc, "w_ba": w_ba, "w_o": w_o.astype(BF16),
            "w_gate": w_gate.astype(BF16), "w_up": w_up.astype(BF16), "w_down": w_down.astype(BF16)}


def _layer_vectors(l, norm_mix_pre, conv_qkv_w, a_log, dt_bias, gdn_norm_w, conv_sc_w, norm_mix_post,
                   norm_ffn_pre, norm_ffn_post):
    lane_row = lambda v: jnp.zeros((1, LANES), F32).at[0, A_LANE0:A_LANE0 + HEADS].set(v.astype(F32))
    return {
        "norm_mix_pre": norm_mix_pre[l].reshape(1, D_MODEL),
        "conv_qkv_w": conv_qkv_w[l],
        "a_log": lane_row(a_log[l]),
        "dt_bias": lane_row(dt_bias[l]),
        "gdn_norm_w": gdn_norm_w[l].reshape(1, HEAD_DIM),
        "conv_sc_w": conv_sc_w[l],
        "norm_mix_post": norm_mix_post[l].reshape(1, D_MODEL),
        "norm_ffn_pre": norm_ffn_pre[l].reshape(1, D_MODEL),
        "norm_ffn_post": norm_ffn_post[l].reshape(1, D_MODEL),
    }


def _mixer_tiling(batch, seq):
    chunk = min(CHUNK, seq)
    if seq <= CHUNK:
        return batch, seq, chunk
    return batch, MIXER_TILE, chunk


def _run(x, conv0, s0, sc0, params, stacks):
    batch, seq, _ = x.shape
    nb, tile, chunk = _mixer_tiling(batch, seq)
    ffn_tile = min(FFN_TILE, batch * seq)
    convs, states, scs = [], [], []
    for l, p in enumerate(params):
        x, c, s, sc = _mixer(x, conv0, s0, sc0, p, stacks, l, nb=nb, tile=tile, chunk=chunk)
        x = _ffn(x.reshape(batch * seq, D_MODEL), p, stacks, l, tile=ffn_tile).reshape(batch, seq, D_MODEL)
        convs.append(c)
        states.append(s)
        scs.append(sc)
    return x, jnp.stack(convs), jnp.stack(states), jnp.stack(scs)


def kernel(x_prompt, x_sample, cache_gdn_conv, state_gdn, cache_sc_conv, norm_mix_pre, w_in, conv_qkv_w, a_log, dt_bias, gdn_norm_w, conv_sc_w, w_o, norm_mix_post, norm_ffn_pre, w_gate, w_up, w_down, norm_ffn_post):
    depth = w_in.shape[0]
    stacks = _weight_stacks(w_in, w_o, w_gate, w_up, w_down)
    params = [_layer_vectors(l, norm_mix_pre, conv_qkv_w, a_log, dt_bias, gdn_norm_w, conv_sc_w,
                             norm_mix_post, norm_ffn_pre, norm_ffn_post) for l in range(depth)]
    bp = x_prompt.shape[0]
    zc = jnp.zeros((depth, bp, GDN_CONV - 1, QKV_W), F32)
    zs = jnp.zeros((depth, bp, HEADS, HEAD_DIM, HEAD_DIM), F32)
    zsc = jnp.zeros((depth, bp, SC_CONV - 1, SC_W), F32)
    y_prompt, conv_p, state_p, sc_p = _run(x_prompt, zc, zs, zsc, params, stacks)
    y_sample, conv_s, state_s, sc_s = _run(x_sample, cache_gdn_conv, state_gdn, cache_sc_conv, params, stacks)
    return (y_prompt, y_sample, conv_p, state_p, sc_p, conv_s, state_s, sc_s)
```

```python
import functools

import jax
import jax.numpy as jnp
from jax import lax
from jax.experimental import pallas as pl
from jax.experimental.pallas import tpu as pltpu

F32 = jnp.float32
BF16 = jnp.bfloat16

D_MODEL = 1024
HEADS = 4
HEAD_DIM = 128
GDN_W = HEADS * HEAD_DIM
SC_W = D_MODEL - GDN_W
QKV_W = 3 * GDN_W
GDN_CONV = 4
SC_CONV = 3
D_FF = 2816
CHUNK = 64
EPS = 1e-6

QKVZ_W = QKV_W + GDN_W
GATE_COLS = 2 * HEADS
W_IN_CAST_COLS = 256
LANES = 128
SUBLANES = 8
A_LANE0 = HEADS

VMEM_LIMIT_BYTES = 56 * 1024 * 1024
MXU_TILE = 256
FFN_SLABS = (3 * MXU_TILE,) * 3 + (2 * MXU_TILE,)
assert sum(FFN_SLABS) == D_FF
FFN_TILE = 1024
MIXER_TILE = 512
PREP_CHUNKS = 4
MAX_UNROLLED_TRIPS = 8


def _rms(x, g):
    return x * lax.rsqrt(jnp.mean(x * x, axis=-1, keepdims=True) + EPS) * g


def _silu(x):
    return x * jax.nn.sigmoid(x)


def _mm(a, b):
    return jnp.dot(a, b, preferred_element_type=F32)


def _aligned(index, multiple):
    return index if isinstance(index, int) else pl.multiple_of(index, multiple)


def _loop(trips, body):
    if trips <= MAX_UNROLLED_TRIPS:
        for i in range(trips):
            body(i, 0)
    else:
        lax.fori_loop(0, trips, body, 0)


def _split2(x):
    hi = x.astype(BF16)
    lo = (x - hi.astype(F32)).astype(BF16)
    return hi, lo


def _split3(x):
    h1 = x.astype(BF16)
    r1 = x - h1.astype(F32)
    h2 = r1.astype(BF16)
    h3 = (r1 - h2.astype(F32)).astype(BF16)
    return h1, h2, h3


def _unit_lower_inverses(a_strict, row, col, size):
    eye = (row == col).astype(F32)
    ts = [eye - jnp.where((row >> 1) == (col >> 1), a, 0.0) for a in a_strict]
    a16 = [a.astype(BF16) for a in a_strict]
    shift = 1
    while (1 << shift) < size:
        off = ((row >> (shift + 1)) == (col >> (shift + 1))) & ((row >> shift) != (col >> shift))
        t16 = [t.astype(BF16) for t in ts]
        inner = [_mm(jnp.where(off, a, jnp.zeros_like(a)), t) for a, t in zip(a16, t16)]
        outer = [_mm(t, i.astype(BF16)) for t, i in zip(t16, inner)]
        ts = [t - o for t, o in zip(ts, outer)]
        shift += 1
    return ts


def _solve_refined(a_strict, t_approx, rhs):
    t16 = [t.astype(BF16) for t in t_approx]
    x0 = [_mm(t, r.astype(BF16)).astype(BF16) for t, r in zip(t16, rhs)]
    a_parts = [_split2(a) for a in a_strict]
    ax_hi = [_mm(a[0], x) for a, x in zip(a_parts, x0)]
    ax_lo = [_mm(a[1], x) for a, x in zip(a_parts, x0)]
    res = [r - x.astype(F32) - (h + l) for r, x, h, l in zip(rhs, x0, ax_hi, ax_lo)]
    corr = [_mm(t, r.astype(BF16)) for t, r in zip(t16, res)]
    return [x.astype(F32) + c for x, c in zip(x0, corr)]


def _causal_dwconv(head_ref, b, x, w, tail_ref):
    width = w.shape[0]
    n = x.shape[0]
    xp = jnp.concatenate([head_ref[b], x], axis=0)
    acc = pltpu.roll(xp, width - 1, axis=0)[SUBLANES:] * w[0:1]
    for i in range(1, width - 1):
        acc = acc + pltpu.roll(xp, width - 1 - i, axis=0)[SUBLANES:] * w[i:i + 1]
    acc = acc + x * w[width - 1:width]
    head_ref[b] = xp[n:]
    tail_ref[b] = xp[n + SUBLANES - (width - 1):]
    return acc


def _mixer_kernel(x_ref, cprev_ref, s0_ref, scprev_ref, gpre_ref, wqkvz_ref, wsc_ref, wba_ref, cw_ref, alog_ref,
                  dtb_ref, gnw_ref, scw_ref, wo_ref, gpost_ref,
                  y_ref, cnew_ref, snew_ref, scnew_ref,
                  s_scr, chead, schead, q_scr, k_scr, v_scr, g_scr, beta_scr, o_scr,
                  gc_scr, u_scr, wq_scr, qkk_scr, zg_scr, sc_scr,
                  *, nb, tile, chunk):
    rows = nb * tile

    @pl.when(pl.program_id(1) == 0)
    def _():
        s_scr[...] = s0_ref[...]
        chead[...] = jnp.zeros(chead.shape, F32)
        schead[...] = jnp.zeros(schead.shape, F32)
        for b in range(nb):
            chead[b, SUBLANES - (GDN_CONV - 1):, :] = cprev_ref[b]
            schead[b, SUBLANES - (SC_CONV - 1):, :] = scprev_ref[b]

    x = x_ref[...].reshape(rows, D_MODEL)
    h = _rms(x, gpre_ref[...]).astype(BF16)

    qkv_in = _mm(h, wqkvz_ref[:, :QKV_W])
    pba = _mm(h, wba_ref[...])
    z_in = _mm(h, wqkvz_ref[:, QKV_W:])
    sc_c = _mm(h, wsc_ref[:, SC_W:2 * SC_W])
    sc_h = _mm(h, wsc_ref[:, 2 * SC_W:])
    sc_b = _mm(h, wsc_ref[:, :SC_W])

    cw = cw_ref[...]
    for b in range(nb):
        r0 = b * tile
        acc = _causal_dwconv(chead, b, qkv_in[r0:r0 + tile], cw, cnew_ref)
        act = _silu(acc)
        for hh in range(HEADS):
            lo = hh * HEAD_DIM
            qh = act[:, lo:lo + HEAD_DIM]
            kh = act[:, GDN_W + lo:GDN_W + lo + HEAD_DIM]
            q_scr[r0:r0 + tile, lo:lo + HEAD_DIM] = (
                qh * lax.rsqrt(jnp.sum(qh * qh, axis=-1, keepdims=True) + EPS) * (HEAD_DIM ** -0.5))
            k_scr[r0:r0 + tile, lo:lo + HEAD_DIM] = (
                kh * lax.rsqrt(jnp.sum(kh * kh, axis=-1, keepdims=True) + EPS))
        v_scr[r0:r0 + tile, :] = act[:, 2 * GDN_W:]

    beta_scr[...] = jax.nn.sigmoid(pba)
    ga = pba + dtb_ref[...]
    softplus = jnp.maximum(ga, 0.0) + jnp.log1p(jnp.exp(-jnp.abs(ga)))
    g_scr[...] = -jnp.exp(alog_ref[...]) * softplus

    row = lax.broadcasted_iota(jnp.int32, (chunk, chunk), 0)
    col = lax.broadcasted_iota(jnp.int32, (chunk, chunk), 1)
    causal = row >= col
    strict = row > col
    tri = causal.astype(BF16)

    zg_scr[...] = _silu(z_in)
    cm = sc_c * sc_h
    scw = scw_ref[...]
    for b in range(nb):
        r0 = b * tile
        sc_scr[r0:r0 + tile, :] = sc_b[r0:r0 + tile] * _causal_dwconv(schead, b, cm[r0:r0 + tile], scw, scnew_ref)

    n_chunks = tile // chunk
    cpi = min(PREP_CHUNKS, n_chunks)
    units = [(b, k) for b in range(nb) for k in range(cpi)]
    prep_probs = [(ui, hh) for ui in range(len(units)) for hh in range(HEADS)]
    chain_probs = [(b, hh) for b in range(nb) for hh in range(HEADS)]

    def prep_step(it, carry):
        cidx = [it * cpi + k for _, k in units]
        r0s = [_aligned(b * tile + ci * chunk, chunk) for (b, _), ci in zip(units, cidx)]
        g3 = [_split3(g_scr[pl.ds(r0, chunk), :]) for r0 in r0s]
        gcs = [_mm(tri, t[0]) + (_mm(tri, t[1]) + _mm(tri, t[2])) for t in g3]
        gcts = [gc.T for gc in gcs]
        betas = [beta_scr[pl.ds(r0, chunk), :] for r0 in r0s]
        for r0, gc in zip(r0s, gcs):
            gc_scr[pl.ds(r0, chunk), :] = gc
        ops = []
        for ui, hh in prep_probs:
            lo = hh * HEAD_DIM
            gcol = gcs[ui][:, A_LANE0 + hh:A_LANE0 + hh + 1]
            grow = gcts[ui][A_LANE0 + hh:A_LANE0 + hh + 1, :]
            bcol = betas[ui][:, hh:hh + 1]
            qh = q_scr[pl.ds(r0s[ui], chunk), lo:lo + HEAD_DIM]
            kh = k_scr[pl.ds(r0s[ui], chunk), lo:lo + HEAD_DIM]
            vh = v_scr[pl.ds(r0s[ui], chunk), lo:lo + HEAD_DIM]
            kb = kh * bcol
            eg = jnp.exp(gcol)
            k_t = kh.T
            ops.append(dict(
                decay=jnp.exp(jnp.where(causal, gcol - grow, -jnp.inf)),
                kbq16=jnp.concatenate([kb, qh], axis=0).astype(BF16), kt16=k_t.astype(BF16),
                rhs=jnp.concatenate([vh * bcol, kb * eg], axis=1),
                qg=qh * eg,
                k_tail_t=k_t * jnp.exp(gcol[chunk - 1:chunk, :] - grow)))
        kq = [_mm(o["kbq16"], o["kt16"]) for o in ops]
        a_mats = [jnp.where(strict, p[:chunk] * o["decay"], 0.0) for p, o in zip(kq, ops)]
        qks = [p[chunk:] * o["decay"] for p, o in zip(kq, ops)]
        sols = _solve_refined(a_mats, _unit_lower_inverses(a_mats, row, col, chunk), [o["rhs"] for o in ops])
        for (ui, hh), o, qk, sol in zip(prep_probs, ops, qks, sols):
            lo = hh * HEAD_DIM
            r0 = r0s[ui]
            u_scr[pl.ds(r0, chunk), lo:lo + HEAD_DIM] = sol[:, :HEAD_DIM]
            wq_scr[pl.ds(2 * r0, 2 * chunk), lo:lo + HEAD_DIM] = jnp.concatenate(
                [sol[:, HEAD_DIM:], o["qg"]], axis=0).astype(BF16)
            qkk_scr[(units[ui][0] * n_chunks + cidx[ui]) * HEADS + hh] = jnp.concatenate(
                [qk, o["k_tail_t"]], axis=0).astype(BF16)
        return carry

    def chain_step(c, carry):
        r0s = [_aligned(b * tile + c * chunk, chunk) for b in range(nb)]
        s_old = [s_scr[b, hh] for b, hh in chain_probs]
        ws = [_mm(wq_scr[pl.ds(2 * r0s[b], 2 * chunk), hh * HEAD_DIM:(hh + 1) * HEAD_DIM], s.astype(BF16))
              for (b, hh), s in zip(chain_probs, s_old)]
        v_new = [(u_scr[pl.ds(r0s[b], chunk), hh * HEAD_DIM:(hh + 1) * HEAD_DIM] - w[:chunk]).astype(BF16)
                 for (b, hh), w in zip(chain_probs, ws)]
        r2 = [_mm(qkk_scr[(b * n_chunks + c) * HEADS + hh], v) for (b, hh), v in zip(chain_probs, v_new)]
        for (b, hh), s, w, r in zip(chain_probs, s_old, ws, r2):
            lo = hh * HEAD_DIM
            o_scr[pl.ds(r0s[b], chunk), lo:lo + HEAD_DIM] = w[chunk:] + r[:chunk]
            glast = gc_scr[pl.ds(r0s[b] + (chunk - 1), 1), :][:, A_LANE0 + hh:A_LANE0 + hh + 1]
            s_scr[b, hh] = s * jnp.exp(glast) + r[chunk:]
        return carry

    if n_chunks <= MAX_UNROLLED_TRIPS:
        for it in range(n_chunks // cpi):
            prep_step(it, 0)
            for c in range(it * cpi, (it + 1) * cpi):
                chain_step(c, 0)
    else:
        _loop(n_chunks // cpi, prep_step)
        _loop(n_chunks, chain_step)
    snew_ref[...] = s_scr[...]

    gnw = gnw_ref[...]
    og = []
    for hh in range(HEADS):
        lo = hh * HEAD_DIM
        og.append(_rms(o_scr[:, lo:lo + HEAD_DIM], gnw) * zg_scr[:, lo:lo + HEAD_DIM])
    mix = _mm(jnp.concatenate(og + [sc_scr[...]], axis=1).astype(BF16), wo_ref[...])
    y_ref[...] = (x_ref[...].reshape(rows, D_MODEL) + _rms(mix, gpost_ref[...])).reshape(nb, tile, D_MODEL)


def _ffn_kernel(x_ref, gpre_ref, wg_ref, wu_ref, wd_ref, gpost_ref, y_ref):
    x = x_ref[...]
    h = _rms(x, gpre_ref[...]).astype(BF16)
    f = None
    lo = 0
    for width in FFN_SLABS:
        act = (_silu(_mm(h, wg_ref[:, lo:lo + width])) * _mm(h, wu_ref[:, lo:lo + width])).astype(BF16)
        part = _mm(act, wd_ref[lo:lo + width, :])
        f = part if f is None else f + part
        lo += width
    y_ref[...] = x + _rms(f, gpost_ref[...])


def _const_spec(shape):
    return pl.BlockSpec(shape, lambda *_: (0,) * len(shape))


def _layer_spec(layer, shape):
    return pl.BlockSpec((None,) + shape, lambda *_: (layer, 0, 0))


def _mixer(x, conv_prev, s0, sc_prev, p, stacks, layer, *, nb, tile, chunk):
    batch, seq, _ = x.shape
    rows = nb * tile
    grid = (batch // nb, seq // tile)
    per_b = lambda i, j: (i, 0, 0)
    in_specs = [
        pl.BlockSpec((nb, tile, D_MODEL), lambda i, j: (i, j, 0)),
        pl.BlockSpec((None, nb, GDN_CONV - 1, QKV_W), lambda i, j: (layer, i, 0, 0)),
        pl.BlockSpec((None, nb, HEADS, HEAD_DIM, HEAD_DIM), lambda i, j: (layer, i, 0, 0, 0)),
        pl.BlockSpec((None, nb, SC_CONV - 1, SC_W), lambda i, j: (layer, i, 0, 0)),
        _const_spec((1, D_MODEL)),
        _layer_spec(layer, (D_MODEL, QKVZ_W)),
        _layer_spec(layer, (D_MODEL, 3 * SC_W)),
        _layer_spec(layer, (D_MODEL, LANES)),
        _const_spec((GDN_CONV, QKV_W)),
        _const_spec((1, LANES)),
        _const_spec((1, LANES)),
        _const_spec((1, HEAD_DIM)),
        _const_spec((SC_CONV, SC_W)),
        _layer_spec(layer, (D_MODEL, D_MODEL)),
        _const_spec((1, D_MODEL)),
    ]
    out_specs = [
        pl.BlockSpec((nb, tile, D_MODEL), lambda i, j: (i, j, 0)),
        pl.BlockSpec((nb, GDN_CONV - 1, QKV_W), per_b),
        pl.BlockSpec((nb, HEADS, HEAD_DIM, HEAD_DIM), lambda i, j: (i, 0, 0, 0)),
        pl.BlockSpec((nb, SC_CONV - 1, SC_W), per_b),
    ]
    out_shape = [
        jax.ShapeDtypeStruct(x.shape, F32),
        jax.ShapeDtypeStruct(conv_prev.shape[1:], F32),
        jax.ShapeDtypeStruct(s0.shape[1:], F32),
        jax.ShapeDtypeStruct(sc_prev.shape[1:], F32),
    ]
    scratch = [
        pltpu.VMEM((nb, HEADS, HEAD_DIM, HEAD_DIM), F32),
        pltpu.VMEM((nb, SUBLANES, QKV_W), F32),
        pltpu.VMEM((nb, SUBLANES, SC_W), F32),
        pltpu.VMEM((rows, GDN_W), F32),
        pltpu.VMEM((rows, GDN_W), F32),
        pltpu.VMEM((rows, GDN_W), F32),
        pltpu.VMEM((rows, LANES), F32),
        pltpu.VMEM((rows, LANES), F32),
        pltpu.VMEM((rows, GDN_W), F32),
        pltpu.VMEM((rows, LANES), F32),
        pltpu.VMEM((rows, GDN_W), F32),
        pltpu.VMEM((2 * rows, GDN_W), BF16),
        pltpu.VMEM((rows // chunk * HEADS, chunk + HEAD_DIM, chunk), BF16),
        pltpu.VMEM((rows, GDN_W), F32),
        pltpu.VMEM((rows, SC_W), F32),
    ]
    return pl.pallas_call(
        functools.partial(_mixer_kernel, nb=nb, tile=tile, chunk=chunk),
        grid=grid,
        in_specs=in_specs,
        out_specs=out_specs,
        out_shape=out_shape,
        scratch_shapes=scratch,
        compiler_params=pltpu.CompilerParams(
            dimension_semantics=("arbitrary", "arbitrary"),
            vmem_limit_bytes=VMEM_LIMIT_BYTES),
        name=f"mixer_t{tile}_c{chunk}",
    )(x, conv_prev, s0, sc_prev, p["norm_mix_pre"], stacks["w_in"], stacks["w_sc"], stacks["w_ba"],
      p["conv_qkv_w"], p["a_log"],
      p["dt_bias"], p["gdn_norm_w"], p["conv_sc_w"], stacks["w_o"], p["norm_mix_post"])


def _ffn(x2d, p, stacks, layer, *, tile):
    rows = x2d.shape[0]
    row_spec = pl.BlockSpec((tile, D_MODEL), lambda i: (i, 0))
    return pl.pallas_call(
        _ffn_kernel,
        grid=(rows // tile,),
        in_specs=[
            row_spec,
            _const_spec((1, D_MODEL)),
            _layer_spec(layer, (D_MODEL, D_FF)),
            _layer_spec(layer, (D_MODEL, D_FF)),
            _layer_spec(layer, (D_FF, D_MODEL)),
            _const_spec((1, D_MODEL)),
        ],
        out_specs=row_spec,
        out_shape=jax.ShapeDtypeStruct(x2d.shape, F32),
        compiler_params=pltpu.CompilerParams(
            dimension_semantics=("arbitrary",),
            vmem_limit_bytes=VMEM_LIMIT_BYTES),
        name=f"ffn_t{tile}",
    )(x2d, p["norm_ffn_pre"], stacks["w_gate"], stacks["w_up"], stacks["w_down"], p["norm_ffn_post"])


def _w_in_groups_kernel(wq_ref, wg_ref, ws_ref, qkvz_ref, sc_ref, ba_ref):
    qkvz_ref[...] = wq_ref[...].T.astype(BF16)
    sc_ref[...] = ws_ref[0].T.astype(BF16)
    gate = jnp.concatenate([wg_ref[...], jnp.zeros((LANES - GATE_COLS, wg_ref.shape[1]), F32)], axis=0)
    ba_ref[...] = gate.T.astype(BF16)


def _w_in_groups(w_in):
    depth, d_in, _ = w_in.shape
    blk = W_IN_CAST_COLS
    n_q, n_s = QKVZ_W // blk, 3 * SC_W // blk
    sc_block = lambda r: jnp.minimum(r, n_s - 1)
    sc_row0 = QKVZ_W + GATE_COLS
    w_t = jnp.swapaxes(w_in, 1, 2)
    return pl.pallas_call(
        _w_in_groups_kernel,
        grid=(depth, n_q),
        in_specs=[
            pl.BlockSpec((None, blk, d_in), lambda l, r: (l, r, 0)),
            pl.BlockSpec((None, GATE_COLS, d_in), lambda l, r: (l, QKVZ_W // GATE_COLS, 0)),
            pl.BlockSpec((pl.Element(1), pl.Element(blk), pl.Element(d_in)),
                         lambda l, r: (l, pl.multiple_of(sc_row0 + blk * sc_block(r), SUBLANES), 0)),
        ],
        out_specs=[
            pl.BlockSpec((None, d_in, blk), lambda l, r: (l, 0, r)),
            pl.BlockSpec((None, d_in, blk), lambda l, r: (l, 0, sc_block(r))),
            pl.BlockSpec((None, d_in, LANES), lambda l, r: (l, 0, 0)),
        ],
        out_shape=[jax.ShapeDtypeStruct((depth, d_in, QKVZ_W), BF16),
                   jax.ShapeDtypeStruct((depth, d_in, 3 * SC_W), BF16),
                   jax.ShapeDtypeStruct((depth, d_in, LANES), BF16)],
        name="w_in_groups",
    )(w_t, w_t, w_t)


def _weight_stacks(w_in, w_o, w_gate, w_up, w_down):
    w_qkvz, w_sc, w_ba = _w_in_groups(w_in)
    return {"w_in": w_qkvz, "w_sc": w_sc, "w_ba": w_ba, "w_o": w_o.astype(BF16),
            "w_gate": w_gate.astype(BF16), "w_up": w_up.astype(BF16), "w_down": w_down.astype(BF16)}


def _layer_vectors(l, norm_mix_pre, conv_qkv_w, a_log, dt_bias, gdn_norm_w, conv_sc_w, norm_mix_post,
                   norm_ffn_pre, norm_ffn_post):
    lane_row = lambda v: jnp.zeros((1, LANES), F32).at[0, A_LANE0:A_LANE0 + HEADS].set(v.astype(F32))
    return {
        "norm_mix_pre": norm_mix_pre[l].reshape(1, D_MODEL),
        "conv_qkv_w": conv_qkv_w[l],
        "a_log": lane_row(a_log[l]),
        "dt_bias": lane_row(dt_bias[l]),
        "gdn_norm_w": gdn_norm_w[l].reshape(1, HEAD_DIM),
        "conv_sc_w": conv_sc_w[l],
        "norm_mix_post": norm_mix_post[l].reshape(1, D_MODEL),
        "norm_ffn_pre": norm_ffn_pre[l].reshape(1, D_MODEL),
        "norm_ffn_post": norm_ffn_post[l].reshape(1, D_MODEL),
    }


def _mixer_tiling(batch, seq):
    chunk = min(CHUNK, seq)
    if seq <= CHUNK:
        return batch, seq, chunk
    return batch, MIXER_TILE, chunk


def _run(x, conv0, s0, sc0, params, stacks):
    batch, seq, _ = x.shape
    nb, tile, chunk = _mixer_tiling(batch, seq)
    ffn_tile = min(FFN_TILE, batch * seq)
    convs, states, scs = [], [], []
    for l, p in enumerate(params):
        x, c, s, sc = _mixer(x, conv0, s0, sc0, p, stacks, l, nb=nb, tile=tile, chunk=chunk)
        x = _ffn(x.reshape(batch * seq, D_MODEL), p, stacks, l, tile=ffn_tile).reshape(batch, seq, D_MODEL)
        convs.append(c)
        states.append(s)
        scs.append(sc)
    return x, jnp.stack(convs), jnp.stack(states), jnp.stack(scs)


def kernel(x_prompt, x_sample, cache_gdn_conv, state_gdn, cache_sc_conv, norm_mix_pre, w_in, conv_qkv_w, a_log, dt_bias, gdn_norm_w, conv_sc_w, w_o, norm_mix_post, norm_ffn_pre, w_gate, w_up, w_down, norm_ffn_post):
    depth = w_in.shape[0]
    stacks = _weight_stacks(w_in, w_o, w_gate, w_up, w_down)
    params = [_layer_vectors(l, norm_mix_pre, conv_qkv_w, a_log, dt_bias, gdn_norm_w, conv_sc_w,
                             norm_mix_post, norm_ffn_pre, norm_ffn_post) for l in range(depth)]
    bp = x_prompt.shape[0]
    zc = jnp.zeros((depth, bp, GDN_CONV - 1, QKV_W), F32)
    zs = jnp.zeros((depth, bp, HEADS, HEAD_DIM, HEAD_DIM), F32)
    zsc = jnp.zeros((depth, bp, SC_CONV - 1, SC_W), F32)
    y_prompt, conv_p, state_p, sc_p = _run(x_prompt, zc, zs, zsc, params, stacks)
    y_sample, conv_s, state_s, sc_s = _run(x_sample, cache_gdn_conv, state_gdn, cache_sc_conv, params, stacks)
    return (y_prompt, y_sample, conv_p, state_p, sc_p, conv_s, state_s, sc_s)
```

```python
import functools

import jax
import jax.numpy as jnp
from jax import lax
from jax.experimental import pallas as pl
from jax.experimental.pallas import tpu as pltpu

F32 = jnp.float32
BF16 = jnp.bfloat16

D_MODEL = 1024
HEADS = 4
HEAD_DIM = 128
GDN_W = HEADS * HEAD_DIM
SC_W = D_MODEL - GDN_W
QKV_W = 3 * GDN_W
GDN_CONV = 4
SC_CONV = 3
D_FF = 2816
CHUNK = 64
EPS = 1e-6

QKVZ_W = QKV_W + GDN_W
GATE_COLS = 2 * HEADS
W_IN_CAST_COLS = 256
LANES = 128
SUBLANES = 8
A_LANE0 = HEADS

VMEM_LIMIT_BYTES = 56 * 1024 * 1024
MXU_TILE = 256
FFN_SLABS = (6 * MXU_TILE, 5 * MXU_TILE)
assert sum(FFN_SLABS) == D_FF
FFN_TILE = 1024
FFN_ROW_PARTS = 2
FFN_MIN_PART_ROWS = 2 * MXU_TILE
MIXER_TILE = 512
PREP_CHUNKS = 4
MAX_UNROLLED_TRIPS = 8


def _rms(x, g):
    return x * lax.rsqrt(jnp.mean(x * x, axis=-1, keepdims=True) + EPS) * g


def _silu(x):
    return x * jax.nn.sigmoid(x)


def _mm(a, b):
    return jnp.dot(a, b, preferred_element_type=F32)


def _aligned(index, multiple):
    return index if isinstance(index, int) else pl.multiple_of(index, multiple)


def _loop(trips, body):
    if trips <= MAX_UNROLLED_TRIPS:
        for i in range(trips):
            body(i, 0)
    else:
        lax.fori_loop(0, trips, body, 0)


def _split2(x):
    hi = x.astype(BF16)
    lo = (x - hi.astype(F32)).astype(BF16)
    return hi, lo


def _split3(x):
    h1 = x.astype(BF16)
    r1 = x - h1.astype(F32)
    h2 = r1.astype(BF16)
    h3 = (r1 - h2.astype(F32)).astype(BF16)
    return h1, h2, h3


def _unit_lower_inverses(a_strict, row, col, size):
    eye = (row == col).astype(F32)
    ts = [eye - jnp.where((row >> 1) == (col >> 1), a, 0.0) for a in a_strict]
    a16 = [a.astype(BF16) for a in a_strict]
    shift = 1
    while (1 << shift) < size:
        off = ((row >> (shift + 1)) == (col >> (shift + 1))) & ((row >> shift) != (col >> shift))
        t16 = [t.astype(BF16) for t in ts]
        inner = [_mm(jnp.where(off, a, jnp.zeros_like(a)), t) for a, t in zip(a16, t16)]
        outer = [_mm(t, i.astype(BF16)) for t, i in zip(t16, inner)]
        ts = [t - o for t, o in zip(ts, outer)]
        shift += 1
    return ts


def _solve_refined(a_strict, t_approx, rhs):
    t16 = [t.astype(BF16) for t in t_approx]
    x0 = [_mm(t, r.astype(BF16)).astype(BF16) for t, r in zip(t16, rhs)]
    a_parts = [_split2(a) for a in a_strict]
    ax_hi = [_mm(a[0], x) for a, x in zip(a_parts, x0)]
    ax_lo = [_mm(a[1], x) for a, x in zip(a_parts, x0)]
    res = [r - x.astype(F32) - (h + l) for r, x, h, l in zip(rhs, x0, ax_hi, ax_lo)]
    corr = [_mm(t, r.astype(BF16)) for t, r in zip(t16, res)]
    return [x.astype(F32) + c for x, c in zip(x0, corr)]


def _causal_dwconv(head_ref, b, x, w, tail_ref):
    width = w.shape[0]
    n = x.shape[0]
    xp = jnp.concatenate([head_ref[b], x], axis=0)
    acc = pltpu.roll(xp, width - 1, axis=0)[SUBLANES:] * w[0:1]
    for i in range(1, width - 1):
        acc = acc + pltpu.roll(xp, width - 1 - i, axis=0)[SUBLANES:] * w[i:i + 1]
    acc = acc + x * w[width - 1:width]
    head_ref[b] = xp[n:]
    tail_ref[b] = xp[n + SUBLANES - (width - 1):]
    return acc


def _mixer_kernel(x_ref, cprev_ref, s0_ref, scprev_ref, gpre_ref, wqkvz_ref, wsc_ref, wba_ref, cw_ref, alog_ref,
                  dtb_ref, gnw_ref, scw_ref, wo_ref, gpost_ref,
                  y_ref, cnew_ref, snew_ref, scnew_ref,
                  s_scr, chead, schead, q_scr, k_scr, v_scr, g_scr, beta_scr, o_scr,
                  gc_scr, u_scr, wq_scr, qkk_scr, zg_scr, sc_scr,
                  *, nb, tile, chunk):
    rows = nb * tile

    @pl.when(pl.program_id(1) == 0)
    def _():
        s_scr[...] = s0_ref[...]
        chead[...] = jnp.zeros(chead.shape, F32)
        schead[...] = jnp.zeros(schead.shape, F32)
        for b in range(nb):
            chead[b, SUBLANES - (GDN_CONV - 1):, :] = cprev_ref[b]
            schead[b, SUBLANES - (SC_CONV - 1):, :] = scprev_ref[b]

    x = x_ref[...].reshape(rows, D_MODEL)
    h = _rms(x, gpre_ref[...]).astype(BF16)

    qkv_in = _mm(h, wqkvz_ref[:, :QKV_W])
    pba = _mm(h, wba_ref[...])
    z_in = _mm(h, wqkvz_ref[:, QKV_W:])
    sc_c = _mm(h, wsc_ref[:, SC_W:2 * SC_W])
    sc_h = _mm(h, wsc_ref[:, 2 * SC_W:])
    sc_b = _mm(h, wsc_ref[:, :SC_W])

    cw = cw_ref[...]
    for b in range(nb):
        r0 = b * tile
        acc = _causal_dwconv(chead, b, qkv_in[r0:r0 + tile], cw, cnew_ref)
        act = _silu(acc)
        for hh in range(HEADS):
            lo = hh * HEAD_DIM
            qh = act[:, lo:lo + HEAD_DIM]
            kh = act[:, GDN_W + lo:GDN_W + lo + HEAD_DIM]
            q_scr[r0:r0 + tile, lo:lo + HEAD_DIM] = (
                qh * lax.rsqrt(jnp.sum(qh * qh, axis=-1, keepdims=True) + EPS) * (HEAD_DIM ** -0.5))
            k_scr[r0:r0 + tile, lo:lo + HEAD_DIM] = (
                kh * lax.rsqrt(jnp.sum(kh * kh, axis=-1, keepdims=True) + EPS))
        v_scr[r0:r0 + tile, :] = act[:, 2 * GDN_W:]

    beta_scr[...] = jax.nn.sigmoid(pba)
    ga = pba + dtb_ref[...]
    softplus = jnp.maximum(ga, 0.0) + jnp.log1p(jnp.exp(-jnp.abs(ga)))
    g_scr[...] = -jnp.exp(alog_ref[...]) * softplus

    row = lax.broadcasted_iota(jnp.int32, (chunk, chunk), 0)
    col = lax.broadcasted_iota(jnp.int32, (chunk, chunk), 1)
    causal = row >= col
    strict = row > col
    tri = causal.astype(BF16)

    zg_scr[...] = _silu(z_in)
    cm = sc_c * sc_h
    scw = scw_ref[...]
    for b in range(nb):
        r0 = b * tile
        sc_scr[r0:r0 + tile, :] = sc_b[r0:r0 + tile] * _causal_dwconv(schead, b, cm[r0:r0 + tile], scw, scnew_ref)

    n_chunks = tile // chunk
    cpi = min(PREP_CHUNKS, n_chunks)
    units = [(b, k) for b in range(nb) for k in range(cpi)]
    prep_probs = [(ui, hh) for ui in range(len(units)) for hh in range(HEADS)]
    chain_probs = [(b, hh) for b in range(nb) for hh in range(HEADS)]

    def prep_step(it, carry):
        cidx = [it * cpi + k for _, k in units]
        r0s = [_aligned(b * tile + ci * chunk, chunk) for (b, _), ci in zip(units, cidx)]
        g3 = [_split3(g_scr[pl.ds(r0, chunk), :]) for r0 in r0s]
        gcs = [_mm(tri, t[0]) + (_mm(tri, t[1]) + _mm(tri, t[2])) for t in g3]
        gcts = [gc.T for gc in gcs]
        betas = [beta_scr[pl.ds(r0, chunk), :] for r0 in r0s]
        for r0, gc in zip(r0s, gcs):
            gc_scr[pl.ds(r0, chunk), :] = gc
        ops = []
        for ui, hh in prep_probs:
            lo = hh * HEAD_DIM
            gcol = gcs[ui][:, A_LANE0 + hh:A_LANE0 + hh + 1]
            grow = gcts[ui][A_LANE0 + hh:A_LANE0 + hh + 1, :]
            bcol = betas[ui][:, hh:hh + 1]
            qh = q_scr[pl.ds(r0s[ui], chunk), lo:lo + HEAD_DIM]
            kh = k_scr[pl.ds(r0s[ui], chunk), lo:lo + HEAD_DIM]
            vh = v_scr[pl.ds(r0s[ui], chunk), lo:lo + HEAD_DIM]
            kb = kh * bcol
            eg = jnp.exp(gcol)
            k_t = kh.T
            ops.append(dict(
                decay=jnp.exp(jnp.where(causal, gcol - grow, -jnp.inf)),
                kbq16=jnp.concatenate([kb, qh], axis=0).astype(BF16), kt16=k_t.astype(BF16),
                rhs=jnp.concatenate([vh * bcol, kb * eg], axis=1),
                qg=qh * eg,
                k_tail_t=k_t * jnp.exp(gcol[chunk - 1:chunk, :] - grow)))
        kq = [_mm(o["kbq16"], o["kt16"]) for o in ops]
        a_mats = [jnp.where(strict, p[:chunk] * o["decay"], 0.0) for p, o in zip(kq, ops)]
        qks = [p[chunk:] * o["decay"] for p, o in zip(kq, ops)]
        sols = _solve_refined(a_mats, _unit_lower_inverses(a_mats, row, col, chunk), [o["rhs"] for o in ops])
        for (ui, hh), o, qk, sol in zip(prep_probs, ops, qks, sols):
            lo = hh * HEAD_DIM
            r0 = r0s[ui]
            u_scr[pl.ds(r0, chunk), lo:lo + HEAD_DIM] = sol[:, :HEAD_DIM]
            wq_scr[pl.ds(2 * r0, 2 * chunk), lo:lo + HEAD_DIM] = jnp.concatenate(
                [sol[:, HEAD_DIM:], o["qg"]], axis=0).astype(BF16)
            qkk_scr[(units[ui][0] * n_chunks + cidx[ui]) * HEADS + hh] = jnp.concatenate(
                [qk, o["k_tail_t"]], axis=0).astype(BF16)
        return carry

    def chain_step(c, carry):
        r0s = [_aligned(b * tile + c * chunk, chunk) for b in range(nb)]
        s_old = [s_scr[b, hh] for b, hh in chain_probs]
        ws = [_mm(wq_scr[pl.ds(2 * r0s[b], 2 * chunk), hh * HEAD_DIM:(hh + 1) * HEAD_DIM], s.astype(BF16))
              for (b, hh), s in zip(chain_probs, s_old)]
        v_new = [(u_scr[pl.ds(r0s[b], chunk), hh * HEAD_DIM:(hh + 1) * HEAD_DIM] - w[:chunk]).astype(BF16)
                 for (b, hh), w in zip(chain_probs, ws)]
        r2 = [_mm(qkk_scr[(b * n_chunks + c) * HEADS + hh], v) for (b, hh), v in zip(chain_probs, v_new)]
        for (b, hh), s, w, r in zip(chain_probs, s_old, ws, r2):
            lo = hh * HEAD_DIM
            o_scr[pl.ds(r0s[b], chunk), lo:lo + HEAD_DIM] = w[chunk:] + r[:chunk]
            glast = gc_scr[pl.ds(r0s[b] + (chunk - 1), 1), :][:, A_LANE0 + hh:A_LANE0 + hh + 1]
            s_scr[b, hh] = s * jnp.exp(glast) + r[chunk:]
        return carry

    if n_chunks <= MAX_UNROLLED_TRIPS:
        for it in range(n_chunks // cpi):
            prep_step(it, 0)
            for c in range(it * cpi, (it + 1) * cpi):
                chain_step(c, 0)
    else:
        _loop(n_chunks // cpi, prep_step)
        _loop(n_chunks, chain_step)
    snew_ref[...] = s_scr[...]

    gnw = gnw_ref[...]
    og = []
    for hh in range(HEADS):
        lo = hh * HEAD_DIM
        og.append(_rms(o_scr[:, lo:lo + HEAD_DIM], gnw) * zg_scr[:, lo:lo + HEAD_DIM])
    mix = _mm(jnp.concatenate(og + [sc_scr[...]], axis=1).astype(BF16), wo_ref[...])
    y_ref[...] = (x_ref[...].reshape(rows, D_MODEL) + _rms(mix, gpost_ref[...])).reshape(nb, tile, D_MODEL)


def _ffn_kernel(x_ref, gpre_ref, wg_ref, wu_ref, wd_ref, gpost_ref, y_ref):
    rows = x_ref.shape[0]
    parts = FFN_ROW_PARTS if rows >= FFN_ROW_PARTS * FFN_MIN_PART_ROWS else 1
    span = rows // parts
    hs = [_rms(x_ref[i * span:(i + 1) * span, :], gpre_ref[...]).astype(BF16) for i in range(parts)]
    for i, h in enumerate(hs):
        f = None
        lo = 0
        for width in FFN_SLABS:
            act = (_silu(_mm(h, wg_ref[:, lo:lo + width])) * _mm(h, wu_ref[:, lo:lo + width])).astype(BF16)
            part = _mm(act, wd_ref[lo:lo + width, :])
            f = part if f is None else f + part
            lo += width
        y_ref[i * span:(i + 1) * span, :] = x_ref[i * span:(i + 1) * span, :] + _rms(f, gpost_ref[...])


def _const_spec(shape):
    return pl.BlockSpec(shape, lambda *_: (0,) * len(shape))


def _layer_spec(layer, shape):
    return pl.BlockSpec((None,) + shape, lambda *_: (layer, 0, 0))


def _mixer(x, conv_prev, s0, sc_prev, p, stacks, layer, *, nb, tile, chunk):
    batch, seq, _ = x.shape
    rows = nb * tile
    grid = (batch // nb, seq // tile)
    per_b = lambda i, j: (i, 0, 0)
    in_specs = [
        pl.BlockSpec((nb, tile, D_MODEL), lambda i, j: (i, j, 0)),
        pl.BlockSpec((None, nb, GDN_CONV - 1, QKV_W), lambda i, j: (layer, i, 0, 0)),
        pl.BlockSpec((None, nb, HEADS, HEAD_DIM, HEAD_DIM), lambda i, j: (layer, i, 0, 0, 0)),
        pl.BlockSpec((None, nb, SC_CONV - 1, SC_W), lambda i, j: (layer, i, 0, 0)),
        _const_spec((1, D_MODEL)),
        _layer_spec(layer, (D_MODEL, QKVZ_W)),
        _layer_spec(layer, (D_MODEL, 3 * SC_W)),
        _layer_spec(layer, (D_MODEL, LANES)),
        _const_spec((GDN_CONV, QKV_W)),
        _const_spec((1, LANES)),
        _const_spec((1, LANES)),
        _const_spec((1, HEAD_DIM)),
        _const_spec((SC_CONV, SC_W)),
        _layer_spec(layer, (D_MODEL, D_MODEL)),
        _const_spec((1, D_MODEL)),
    ]
    out_specs = [
        pl.BlockSpec((nb, tile, D_MODEL), lambda i, j: (i, j, 0)),
        pl.BlockSpec((nb, GDN_CONV - 1, QKV_W), per_b),
        pl.BlockSpec((nb, HEADS, HEAD_DIM, HEAD_DIM), lambda i, j: (i, 0, 0, 0)),
        pl.BlockSpec((nb, SC_CONV - 1, SC_W), per_b),
    ]
    out_shape = [
        jax.ShapeDtypeStruct(x.shape, F32),
        jax.ShapeDtypeStruct(conv_prev.shape[1:], F32),
        jax.ShapeDtypeStruct(s0.shape[1:], F32),
        jax.ShapeDtypeStruct(sc_prev.shape[1:], F32),
    ]
    scratch = [
        pltpu.VMEM((nb, HEADS, HEAD_DIM, HEAD_DIM), F32),
        pltpu.VMEM((nb, SUBLANES, QKV_W), F32),
        pltpu.VMEM((nb, SUBLANES, SC_W), F32),
        pltpu.VMEM((rows, GDN_W), F32),
        pltpu.VMEM((rows, GDN_W), F32),
        pltpu.VMEM((rows, GDN_W), F32),
        pltpu.VMEM((rows, LANES), F32),
        pltpu.VMEM((rows, LANES), F32),
        pltpu.VMEM((rows, GDN_W), F32),
        pltpu.VMEM((rows, LANES), F32),
        pltpu.VMEM((rows, GDN_W), F32),
        pltpu.VMEM((2 * rows, GDN_W), BF16),
        pltpu.VMEM((rows // chunk * HEADS, chunk + HEAD_DIM, chunk), BF16),
        pltpu.VMEM((rows, GDN_W), F32),
        pltpu.VMEM((rows, SC_W), F32),
    ]
    return pl.pallas_call(
        functools.partial(_mixer_kernel, nb=nb, tile=tile, chunk=chunk),
        grid=grid,
        in_specs=in_specs,
        out_specs=out_specs,
        out_shape=out_shape,
        scratch_shapes=scratch,
        compiler_params=pltpu.CompilerParams(
            dimension_semantics=("arbitrary", "arbitrary"),
            vmem_limit_bytes=VMEM_LIMIT_BYTES),
        name=f"mixer_t{tile}_c{chunk}",
    )(x, conv_prev, s0, sc_prev, p["norm_mix_pre"], stacks["w_in"], stacks["w_sc"], stacks["w_ba"],
      p["conv_qkv_w"], p["a_log"],
      p["dt_bias"], p["gdn_norm_w"], p["conv_sc_w"], stacks["w_o"], p["norm_mix_post"])


def _ffn(x2d, p, stacks, layer, *, tile):
    rows = x2d.shape[0]
    row_spec = pl.BlockSpec((tile, D_MODEL), lambda i: (i, 0))
    return pl.pallas_call(
        _ffn_kernel,
        grid=(rows // tile,),
        in_specs=[
            row_spec,
            _const_spec((1, D_MODEL)),
            _layer_spec(layer, (D_MODEL, D_FF)),
            _layer_spec(layer, (D_MODEL, D_FF)),
            _layer_spec(layer, (D_FF, D_MODEL)),
            _const_spec((1, D_MODEL)),
        ],
        out_specs=row_spec,
        out_shape=jax.ShapeDtypeStruct(x2d.shape, F32),
        compiler_params=pltpu.CompilerParams(
            dimension_semantics=("arbitrary",),
            vmem_limit_bytes=VMEM_LIMIT_BYTES),
        name=f"ffn_t{tile}",
    )(x2d, p["norm_ffn_pre"], stacks["w_gate"], stacks["w_up"], stacks["w_down"], p["norm_ffn_post"])


def _w_in_groups_kernel(wq_ref, wg_ref, ws_ref, qkvz_ref, sc_ref, ba_ref):
    qkvz_ref[...] = wq_ref[...].T.astype(BF16)
    sc_ref[...] = ws_ref[0].T.astype(BF16)
    gate = jnp.concatenate([wg_ref[...], jnp.zeros((LANES - GATE_COLS, wg_ref.shape[1]), F32)], axis=0)
    ba_ref[...] = gate.T.astype(BF16)


def _w_in_groups(w_in):
    depth, d_in, _ = w_in.shape
    blk = W_IN_CAST_COLS
    n_q, n_s = QKVZ_W // blk, 3 * SC_W // blk
    sc_block = lambda r: jnp.minimum(r, n_s - 1)
    sc_row0 = QKVZ_W + GATE_COLS
    w_t = jnp.swapaxes(w_in, 1, 2)
    return pl.pallas_call(
        _w_in_groups_kernel,
        grid=(depth, n_q),
        in_specs=[
            pl.BlockSpec((None, blk, d_in), lambda l, r: (l, r, 0)),
            pl.BlockSpec((None, GATE_COLS, d_in), lambda l, r: (l, QKVZ_W // GATE_COLS, 0)),
            pl.BlockSpec((pl.Element(1), pl.Element(blk), pl.Element(d_in)),
                         lambda l, r: (l, pl.multiple_of(sc_row0 + blk * sc_block(r), SUBLANES), 0)),
        ],
        out_specs=[
            pl.BlockSpec((None, d_in, blk), lambda l, r: (l, 0, r)),
            pl.BlockSpec((None, d_in, blk), lambda l, r: (l, 0, sc_block(r))),
            pl.BlockSpec((None, d_in, LANES), lambda l, r: (l, 0, 0)),
        ],
        out_shape=[jax.ShapeDtypeStruct((depth, d_in, QKVZ_W), BF16),
                   jax.ShapeDtypeStruct((depth, d_in, 3 * SC_W), BF16),
                   jax.ShapeDtypeStruct((depth, d_in, LANES), BF16)],
        name="w_in_groups",
    )(w_t, w_t, w_t)


def _weight_stacks(w_in, w_o, w_gate, w_up, w_down):
    w_qkvz, w_sc, w_ba = _w_in_groups(w_in)
    return {"w_in": w_qkvz, "w_sc": w_sc, "w_ba": w_ba, "w_o": w_o.astype(BF16),
            "w_gate": w_gate.astype(BF16), "w_up": w_up.astype(BF16), "w_down": w_down.astype(BF16)}


def _layer_vectors(l, norm_mix_pre, conv_qkv_w, a_log, dt_bias, gdn_norm_w, conv_sc_w, norm_mix_post,
                   norm_ffn_pre, norm_ffn_post):
    lane_row = lambda v: jnp.zeros((1, LANES), F32).at[0, A_LANE0:A_LANE0 + HEADS].set(v.astype(F32))
    return {
        "norm_mix_pre": norm_mix_pre[l].reshape(1, D_MODEL),
        "conv_qkv_w": conv_qkv_w[l],
        "a_log": lane_row(a_log[l]),
        "dt_bias": lane_row(dt_bias[l]),
        "gdn_norm_w": gdn_norm_w[l].reshape(1, HEAD_DIM),
        "conv_sc_w": conv_sc_w[l],
        "norm_mix_post": norm_mix_post[l].reshape(1, D_MODEL),
        "norm_ffn_pre": norm_ffn_pre[l].reshape(1, D_MODEL),
        "norm_ffn_post": norm_ffn_post[l].reshape(1, D_MODEL),
    }


def _mixer_tiling(batch, seq):
    chunk = min(CHUNK, seq)
    if seq <= CHUNK:
        return batch, seq, chunk
    return batch, MIXER_TILE, chunk


def _run(x, conv0, s0, sc0, params, stacks):
    batch, seq, _ = x.shape
    nb, tile, chunk = _mixer_tiling(batch, seq)
    ffn_tile = min(FFN_TILE, batch * seq)
    convs, states, scs = [], [], []
    for l, p in enumerate(params):
        x, c, s, sc = _mixer(x, conv0, s0, sc0, p, stacks, l, nb=nb, tile=tile, chunk=chunk)
        x = _ffn(x.reshape(batch * seq, D_MODEL), p, stacks, l, tile=ffn_tile).reshape(batch, seq, D_MODEL)
        convs.append(c)
        states.append(s)
        scs.append(sc)
    return x, jnp.stack(convs), jnp.stack(states), jnp.stack(scs)


def kernel(x_prompt, x_sample, cache_gdn_conv, state_gdn, cache_sc_conv, norm_mix_pre, w_in, conv_qkv_w, a_log, dt_bias, gdn_norm_w, conv_sc_w, w_o, norm_mix_post, norm_ffn_pre, w_gate, w_up, w_down, norm_ffn_post):
    depth = w_in.shape[0]
    stacks = _weight_stacks(w_in, w_o, w_gate, w_up, w_down)
    params = [_layer_vectors(l, norm_mix_pre, conv_qkv_w, a_log, dt_bias, gdn_norm_w, conv_sc_w,
                             norm_mix_post, norm_ffn_pre, norm_ffn_post) for l in range(depth)]
    bp = x_prompt.shape[0]
    zc = jnp.zeros((depth, bp, GDN_CONV - 1, QKV_W), F32)
    zs = jnp.zeros((depth, bp, HEADS, HEAD_DIM, HEAD_DIM), F32)
    zsc = jnp.zeros((depth, bp, SC_CONV - 1, SC_W), F32)
    y_prompt, conv_p, state_p, sc_p = _run(x_prompt, zc, zs, zsc, params, stacks)
    y_sample, conv_s, state_s, sc_s = _run(x_sample, cache_gdn_conv, state_gdn, cache_sc_conv, params, stacks)
    return (y_prompt, y_sample, conv_p, state_p, sc_p, conv_s, state_s, sc_s)
```

```python
import functools

import jax
import jax.numpy as jnp
from jax import lax
from jax.experimental import pallas as pl
from jax.experimental.pallas import tpu as pltpu

F32 = jnp.float32
BF16 = jnp.bfloat16

D_MODEL = 1024
HEADS = 4
HEAD_DIM = 128
GDN_W = HEADS * HEAD_DIM
SC_W = D_MODEL - GDN_W
QKV_W = 3 * GDN_W
GDN_CONV = 4
SC_CONV = 3
D_FF = 2816
CHUNK = 64
EPS = 1e-6

QKVZ_W = QKV_W + GDN_W
GATE_COLS = 2 * HEADS
W_IN_CAST_COLS = 256
LANES = 128
SUBLANES = 8
A_LANE0 = HEADS

VMEM_LIMIT_BYTES = 56 * 1024 * 1024
MXU_TILE = 256
FFN_SLABS = (6 * MXU_TILE, 5 * MXU_TILE)
assert sum(FFN_SLABS) == D_FF
FFN_TILE = 1024
FFN_ROW_PARTS = 2
FFN_MIN_PART_ROWS = 2 * MXU_TILE
MIXER_TILE = 512
PREP_CHUNKS = 4
MAX_UNROLLED_TRIPS = 8


def _rms(x, g):
    return x * lax.rsqrt(jnp.mean(x * x, axis=-1, keepdims=True) + EPS) * g


def _silu(x):
    return x * jax.nn.sigmoid(x)


def _mm(a, b):
    return jnp.dot(a, b, preferred_element_type=F32)


def _aligned(index, multiple):
    return index if isinstance(index, int) else pl.multiple_of(index, multiple)


def _loop(trips, body):
    if trips <= MAX_UNROLLED_TRIPS:
        for i in range(trips):
            body(i, 0)
    else:
        lax.fori_loop(0, trips, body, 0)


def _split2(x):
    hi = x.astype(BF16)
    lo = (x - hi.astype(F32)).astype(BF16)
    return hi, lo


def _split3(x):
    h1 = x.astype(BF16)
    r1 = x - h1.astype(F32)
    h2 = r1.astype(BF16)
    h3 = (r1 - h2.astype(F32)).astype(BF16)
    return h1, h2, h3


def _unit_lower_inverses(a_strict, row, col, size):
    eye = (row == col).astype(F32)
    ts = [eye - jnp.where((row >> 1) == (col >> 1), a, 0.0) for a in a_strict]
    a16 = [a.astype(BF16) for a in a_strict]
    shift = 1
    while (1 << shift) < size:
        off = ((row >> (shift + 1)) == (col >> (shift + 1))) & ((row >> shift) != (col >> shift))
        t16 = [t.astype(BF16) for t in ts]
        inner = [_mm(jnp.where(off, a, jnp.zeros_like(a)), t) for a, t in zip(a16, t16)]
        outer = [_mm(t, i.astype(BF16)) for t, i in zip(t16, inner)]
        ts = [t - o for t, o in zip(ts, outer)]
        shift += 1
    return ts


def _solve_refined(a_strict, t_approx, rhs):
    t16 = [t.astype(BF16) for t in t_approx]
    x0 = [_mm(t, r.astype(BF16)).astype(BF16) for t, r in zip(t16, rhs)]
    a_parts = [_split2(a) for a in a_strict]
    ax_hi = [_mm(a[0], x) for a, x in zip(a_parts, x0)]
    ax_lo = [_mm(a[1], x) for a, x in zip(a_parts, x0)]
    res = [r - x.astype(F32) - (h + l) for r, x, h, l in zip(rhs, x0, ax_hi, ax_lo)]
    corr = [_mm(t, r.astype(BF16)) for t, r in zip(t16, res)]
    return [x.astype(F32) + c for x, c in zip(x0, corr)]


def _causal_dwconv(head_ref, b, x, w, tail_ref):
    width = w.shape[0]
    n = x.shape[0]
    xp = jnp.concatenate([head_ref[b], x], axis=0)
    acc = pltpu.roll(xp, width - 1, axis=0)[SUBLANES:] * w[0:1]
    for i in range(1, width - 1):
        acc = acc + pltpu.roll(xp, width - 1 - i, axis=0)[SUBLANES:] * w[i:i + 1]
    acc = acc + x * w[width - 1:width]
    head_ref[b] = xp[n:]
    tail_ref[b] = xp[n + SUBLANES - (width - 1):]
    return acc


def _mixer_kernel(x_ref, cprev_ref, s0_ref, scprev_ref, gpre_ref, wqkvz_ref, wsc_ref, wba_ref, cw_ref, alog_ref,
                  dtb_ref, gnw_ref, scw_ref, wo_ref, gpost_ref,
                  y_ref, cnew_ref, snew_ref, scnew_ref,
                  s_scr, chead, schead, q_scr, k_scr, v_scr, g_scr, beta_scr, o_scr,
                  gc_scr, u_scr, wq_scr, qkk_scr, zg_scr, sc_scr,
                  *, nb, tile, chunk):
    rows = nb * tile

    @pl.when(pl.program_id(1) == 0)
    def _():
        s_scr[...] = s0_ref[...]
        chead[...] = jnp.zeros(chead.shape, F32)
        schead[...] = jnp.zeros(schead.shape, F32)
        for b in range(nb):
            chead[b, SUBLANES - (GDN_CONV - 1):, :] = cprev_ref[b]
            schead[b, SUBLANES - (SC_CONV - 1):, :] = scprev_ref[b]

    x = x_ref[...].reshape(rows, D_MODEL)
    h = _rms(x, gpre_ref[...]).astype(BF16)

    qkv_in = _mm(h, wqkvz_ref[:, :QKV_W])
    pba = _mm(h, wba_ref[...])
    z_in = _mm(h, wqkvz_ref[:, QKV_W:])
    sc_c = _mm(h, wsc_ref[:, SC_W:2 * SC_W])
    sc_h = _mm(h, wsc_ref[:, 2 * SC_W:])
    sc_b = _mm(h, wsc_ref[:, :SC_W])

    cw = cw_ref[...]
    for b in range(nb):
        r0 = b * tile
        acc = _causal_dwconv(chead, b, qkv_in[r0:r0 + tile], cw, cnew_ref)
        act = _silu(acc)
        for hh in range(HEADS):
            lo = hh * HEAD_DIM
            qh = act[:, lo:lo + HEAD_DIM]
            kh = act[:, GDN_W + lo:GDN_W + lo + HEAD_DIM]
            q_scr[r0:r0 + tile, lo:lo + HEAD_DIM] = (
                qh * lax.rsqrt(jnp.sum(qh * qh, axis=-1, keepdims=True) + EPS) * (HEAD_DIM ** -0.5))
            k_scr[r0:r0 + tile, lo:lo + HEAD_DIM] = (
                kh * lax.rsqrt(jnp.sum(kh * kh, axis=-1, keepdims=True) + EPS))
        v_scr[r0:r0 + tile, :] = act[:, 2 * GDN_W:]

    beta_scr[...] = jax.nn.sigmoid(pba)
    ga = pba + dtb_ref[...]
    softplus = jnp.maximum(ga, 0.0) + jnp.log1p(jnp.exp(-jnp.abs(ga)))
    g_scr[...] = -jnp.exp(alog_ref[...]) * softplus

    row = lax.broadcasted_iota(jnp.int32, (chunk, chunk), 0)
    col = lax.broadcasted_iota(jnp.int32, (chunk, chunk), 1)
    causal = row >= col
    strict = row > col
    tri = causal.astype(BF16)

    zg_scr[...] = _silu(z_in)
    cm = sc_c * sc_h
    scw = scw_ref[...]
    for b in range(nb):
        r0 = b * tile
        sc_scr[r0:r0 + tile, :] = sc_b[r0:r0 + tile] * _causal_dwconv(schead, b, cm[r0:r0 + tile], scw, scnew_ref)

    n_chunks = tile // chunk
    cpi = min(PREP_CHUNKS, n_chunks)
    units = [(b, k) for b in range(nb) for k in range(cpi)]
    prep_probs = [(ui, hh) for ui in range(len(units)) for hh in range(HEADS)]
    chain_probs = [(b, hh) for b in range(nb) for hh in range(HEADS)]

    def prep_step(it, carry):
        cidx = [it * cpi + k for _, k in units]
        r0s = [_aligned(b * tile + ci * chunk, chunk) for (b, _), ci in zip(units, cidx)]
        g3 = [_split3(g_scr[pl.ds(r0, chunk), :]) for r0 in r0s]
        gcs = [_mm(tri, t[0]) + (_mm(tri, t[1]) + _mm(tri, t[2])) for t in g3]
        gcts = [gc.T for gc in gcs]
        betas = [beta_scr[pl.ds(r0, chunk), :] for r0 in r0s]
        for r0, gc in zip(r0s, gcs):
            gc_scr[pl.ds(r0, chunk), :] = gc
        ops = []
        for ui, hh in prep_probs:
            lo = hh * HEAD_DIM
            gcol = gcs[ui][:, A_LANE0 + hh:A_LANE0 + hh + 1]
            grow = gcts[ui][A_LANE0 + hh:A_LANE0 + hh + 1, :]
            bcol = betas[ui][:, hh:hh + 1]
            qh = q_scr[pl.ds(r0s[ui], chunk), lo:lo + HEAD_DIM]
            kh = k_scr[pl.ds(r0s[ui], chunk), lo:lo + HEAD_DIM]
            vh = v_scr[pl.ds(r0s[ui], chunk), lo:lo + HEAD_DIM]
            kb = kh * bcol
            eg = jnp.exp(gcol)
            k_t = kh.T
            ops.append(dict(
                decay=jnp.exp(jnp.where(causal, gcol - grow, -jnp.inf)),
                kbq16=jnp.concatenate([kb, qh], axis=0).astype(BF16), kt16=k_t.astype(BF16),
                rhs=jnp.concatenate([vh * bcol, kb * eg], axis=1),
                qg=qh * eg,
                k_tail_t=k_t * jnp.exp(gcol[chunk - 1:chunk, :] - grow)))
        kq = [_mm(o["kbq16"], o["kt16"]) for o in ops]
        a_mats = [jnp.where(strict, p[:chunk] * o["decay"], 0.0) for p, o in zip(kq, ops)]
        qks = [p[chunk:] * o["decay"] for p, o in zip(kq, ops)]
        sols = _solve_refined(a_mats, _unit_lower_inverses(a_mats, row, col, chunk), [o["rhs"] for o in ops])
        for (ui, hh), o, qk, sol in zip(prep_probs, ops, qks, sols):
            lo = hh * HEAD_DIM
            r0 = r0s[ui]
            u_scr[pl.ds(r0, chunk), lo:lo + HEAD_DIM] = sol[:, :HEAD_DIM]
            wq_scr[pl.ds(2 * r0, 2 * chunk), lo:lo + HEAD_DIM] = jnp.concatenate(
                [sol[:, HEAD_DIM:], o["qg"]], axis=0).astype(BF16)
            qkk_scr[(units[ui][0] * n_chunks + cidx[ui]) * HEADS + hh] = jnp.concatenate(
                [qk, o["k_tail_t"]], axis=0).astype(BF16)
        return carry

    def chain_step(c, carry):
        r0s = [_aligned(b * tile + c * chunk, chunk) for b in range(nb)]
        s_old = [s_scr[b, hh] for b, hh in chain_probs]
        ws = [_mm(wq_scr[pl.ds(2 * r0s[b], 2 * chunk), hh * HEAD_DIM:(hh + 1) * HEAD_DIM], s.astype(BF16))
              for (b, hh), s in zip(chain_probs, s_old)]
        v_new = [(u_scr[pl.ds(r0s[b], chunk), hh * HEAD_DIM:(hh + 1) * HEAD_DIM] - w[:chunk]).astype(BF16)
                 for (b, hh), w in zip(chain_probs, ws)]
        r2 = [_mm(qkk_scr[(b * n_chunks + c) * HEADS + hh], v) for (b, hh), v in zip(chain_probs, v_new)]
        for (b, hh), s, w, r in zip(chain_probs, s_old, ws, r2):
            lo = hh * HEAD_DIM
            o_scr[pl.ds(r0s[b], chunk), lo:lo + HEAD_DIM] = w[chunk:] + r[:chunk]
            glast = gc_scr[pl.ds(r0s[b] + (chunk - 1), 1), :][:, A_LANE0 + hh:A_LANE0 + hh + 1]
            s_scr[b, hh] = s * jnp.exp(glast) + r[chunk:]
        return carry

    if n_chunks <= MAX_UNROLLED_TRIPS:
        for it in range(n_chunks // cpi):
            prep_step(it, 0)
            for c in range(it * cpi, (it + 1) * cpi):
                chain_step(c, 0)
    else:
        _loop(n_chunks // cpi, prep_step)
        _loop(n_chunks, chain_step)
    snew_ref[...] = s_scr[...]

    gnw = gnw_ref[...]
    og = []
    for hh in range(HEADS):
        lo = hh * HEAD_DIM
        og.append(_rms(o_scr[:, lo:lo + HEAD_DIM], gnw) * zg_scr[:, lo:lo + HEAD_DIM])
    mix = _mm(jnp.concatenate(og + [sc_scr[...]], axis=1).astype(BF16), wo_ref[...])
    y_ref[...] = (x_ref[...].reshape(rows, D_MODEL) + _rms(mix, gpost_ref[...])).reshape(nb, tile, D_MODEL)


def _ffn_rows(x_ref, y_ref, gpre_ref, wg_ref, wu_ref, wd_ref, gpost_ref):
    rows = x_ref.shape[0]
    parts = FFN_ROW_PARTS if rows >= FFN_ROW_PARTS * FFN_MIN_PART_ROWS else 1
    span = rows // parts
    hs = [_rms(x_ref[i * span:(i + 1) * span, :], gpre_ref[...]).astype(BF16) for i in range(parts)]
    for i, h in enumerate(hs):
        f = None
        lo = 0
        for width in FFN_SLABS:
            act = (_silu(_mm(h, wg_ref[:, lo:lo + width])) * _mm(h, wu_ref[:, lo:lo + width])).astype(BF16)
            part = _mm(act, wd_ref[lo:lo + width, :])
            f = part if f is None else f + part
            lo += width
        y_ref[i * span:(i + 1) * span, :] = x_ref[i * span:(i + 1) * span, :] + _rms(f, gpost_ref[...])


def _ffn_kernel(x_ref, xe_ref, gpre_ref, wg_ref, wu_ref, wd_ref, gpost_ref, y_ref, ye_ref, *, main_steps):
    weights = (gpre_ref, wg_ref, wu_ref, wd_ref, gpost_ref)
    step = pl.program_id(0)

    @pl.when(step < main_steps)
    def _():
        _ffn_rows(x_ref, y_ref, *weights)

    @pl.when(step == main_steps)
    def _():
        _ffn_rows(xe_ref, ye_ref, *weights)


def _const_spec(shape):
    return pl.BlockSpec(shape, lambda *_: (0,) * len(shape))


def _layer_spec(layer, shape):
    return pl.BlockSpec((None,) + shape, lambda *_: (layer, 0, 0))


def _mixer(x, conv_prev, s0, sc_prev, p, stacks, layer, *, nb, tile, chunk):
    batch, seq, _ = x.shape
    rows = nb * tile
    grid = (batch // nb, seq // tile)
    per_b = lambda i, j: (i, 0, 0)
    in_specs = [
        pl.BlockSpec((nb, tile, D_MODEL), lambda i, j: (i, j, 0)),
        pl.BlockSpec((None, nb, GDN_CONV - 1, QKV_W), lambda i, j: (layer, i, 0, 0)),
        pl.BlockSpec((None, nb, HEADS, HEAD_DIM, HEAD_DIM), lambda i, j: (layer, i, 0, 0, 0)),
        pl.BlockSpec((None, nb, SC_CONV - 1, SC_W), lambda i, j: (layer, i, 0, 0)),
        _const_spec((1, D_MODEL)),
        _layer_spec(layer, (D_MODEL, QKVZ_W)),
        _layer_spec(layer, (D_MODEL, 3 * SC_W)),
        _layer_spec(layer, (D_MODEL, LANES)),
        _const_spec((GDN_CONV, QKV_W)),
        _const_spec((1, LANES)),
        _const_spec((1, LANES)),
        _const_spec((1, HEAD_DIM)),
        _const_spec((SC_CONV, SC_W)),
        _layer_spec(layer, (D_MODEL, D_MODEL)),
        _const_spec((1, D_MODEL)),
    ]
    out_specs = [
        pl.BlockSpec((nb, tile, D_MODEL), lambda i, j: (i, j, 0)),
        pl.BlockSpec((nb, GDN_CONV - 1, QKV_W), per_b),
        pl.BlockSpec((nb, HEADS, HEAD_DIM, HEAD_DIM), lambda i, j: (i, 0, 0, 0)),
        pl.BlockSpec((nb, SC_CONV - 1, SC_W), per_b),
    ]
    out_shape = [
        jax.ShapeDtypeStruct(x.shape, F32),
        jax.ShapeDtypeStruct(conv_prev.shape[1:], F32),
        jax.ShapeDtypeStruct(s0.shape[1:], F32),
        jax.ShapeDtypeStruct(sc_prev.shape[1:], F32),
    ]
    scratch = [
        pltpu.VMEM((nb, HEADS, HEAD_DIM, HEAD_DIM), F32),
        pltpu.VMEM((nb, SUBLANES, QKV_W), F32),
        pltpu.VMEM((nb, SUBLANES, SC_W), F32),
        pltpu.VMEM((rows, GDN_W), F32),
        pltpu.VMEM((rows, GDN_W), F32),
        pltpu.VMEM((rows, GDN_W), F32),
        pltpu.VMEM((rows, LANES), F32),
        pltpu.VMEM((rows, LANES), F32),
        pltpu.VMEM((rows, GDN_W), F32),
        pltpu.VMEM((rows, LANES), F32),
        pltpu.VMEM((rows, GDN_W), F32),
        pltpu.VMEM((2 * rows, GDN_W), BF16),
        pltpu.VMEM((rows // chunk * HEADS, chunk + HEAD_DIM, chunk), BF16),
        pltpu.VMEM((rows, GDN_W), F32),
        pltpu.VMEM((rows, SC_W), F32),
    ]
    return pl.pallas_call(
        functools.partial(_mixer_kernel, nb=nb, tile=tile, chunk=chunk),
        grid=grid,
        in_specs=in_specs,
        out_specs=out_specs,
        out_shape=out_shape,
        scratch_shapes=scratch,
        compiler_params=pltpu.CompilerParams(
            dimension_semantics=("arbitrary", "arbitrary"),
            vmem_limit_bytes=VMEM_LIMIT_BYTES),
        name=f"mixer_t{tile}_c{chunk}",
    )(x, conv_prev, s0, sc_prev, p["norm_mix_pre"], stacks["w_in"], stacks["w_sc"], stacks["w_ba"],
      p["conv_qkv_w"], p["a_log"],
      p["dt_bias"], p["gdn_norm_w"], p["conv_sc_w"], stacks["w_o"], p["norm_mix_post"])


def _ffn(x2d, xe2d, p, stacks, layer, *, tile):
    main_steps = x2d.shape[0] // tile
    row_spec = pl.BlockSpec((tile, D_MODEL), lambda i: (jnp.minimum(i, main_steps - 1), 0))
    extra_spec = _const_spec(xe2d.shape)
    return pl.pallas_call(
        functools.partial(_ffn_kernel, main_steps=main_steps),
        grid=(main_steps + 1,),
        in_specs=[
            row_spec,
            extra_spec,
            _const_spec((1, D_MODEL)),
            _layer_spec(layer, (D_MODEL, D_FF)),
            _layer_spec(layer, (D_MODEL, D_FF)),
            _layer_spec(layer, (D_FF, D_MODEL)),
            _const_spec((1, D_MODEL)),
        ],
        out_specs=[row_spec, extra_spec],
        out_shape=[jax.ShapeDtypeStruct(x2d.shape, F32), jax.ShapeDtypeStruct(xe2d.shape, F32)],
        compiler_params=pltpu.CompilerParams(
            dimension_semantics=("arbitrary",),
            vmem_limit_bytes=VMEM_LIMIT_BYTES),
        name=f"ffn_t{tile}",
    )(x2d, xe2d, p["norm_ffn_pre"], stacks["w_gate"], stacks["w_up"], stacks["w_down"], p["norm_ffn_post"])


def _w_in_groups_kernel(wq_ref, wg_ref, ws_ref, qkvz_ref, sc_ref, ba_ref):
    qkvz_ref[...] = wq_ref[...].T.astype(BF16)
    sc_ref[...] = ws_ref[0].T.astype(BF16)
    gate = jnp.concatenate([wg_ref[...], jnp.zeros((LANES - GATE_COLS, wg_ref.shape[1]), F32)], axis=0)
    ba_ref[...] = gate.T.astype(BF16)


def _w_in_groups(w_in):
    depth, d_in, _ = w_in.shape
    blk = W_IN_CAST_COLS
    n_q, n_s = QKVZ_W // blk, 3 * SC_W // blk
    sc_block = lambda r: jnp.minimum(r, n_s - 1)
    sc_row0 = QKVZ_W + GATE_COLS
    w_t = jnp.swapaxes(w_in, 1, 2)
    return pl.pallas_call(
        _w_in_groups_kernel,
        grid=(depth, n_q),
        in_specs=[
            pl.BlockSpec((None, blk, d_in), lambda l, r: (l, r, 0)),
            pl.BlockSpec((None, GATE_COLS, d_in), lambda l, r: (l, QKVZ_W // GATE_COLS, 0)),
            pl.BlockSpec((pl.Element(1), pl.Element(blk), pl.Element(d_in)),
                         lambda l, r: (l, pl.multiple_of(sc_row0 + blk * sc_block(r), SUBLANES), 0)),
        ],
        out_specs=[
            pl.BlockSpec((None, d_in, blk), lambda l, r: (l, 0, r)),
            pl.BlockSpec((None, d_in, blk), lambda l, r: (l, 0, sc_block(r))),
            pl.BlockSpec((None, d_in, LANES), lambda l, r: (l, 0, 0)),
        ],
        out_shape=[jax.ShapeDtypeStruct((depth, d_in, QKVZ_W), BF16),
                   jax.ShapeDtypeStruct((depth, d_in, 3 * SC_W), BF16),
                   jax.ShapeDtypeStruct((depth, d_in, LANES), BF16)],
        name="w_in_groups",
    )(w_t, w_t, w_t)


def _weight_stacks(w_in, w_o, w_gate, w_up, w_down):
    w_qkvz, w_sc, w_ba = _w_in_groups(w_in)
    return {"w_in": w_qkvz, "w_sc": w_sc, "w_ba": w_ba, "w_o": w_o.astype(BF16),
            "w_gate": w_gate.astype(BF16), "w_up": w_up.astype(BF16), "w_down": w_down.astype(BF16)}


def _layer_vectors(l, norm_mix_pre, conv_qkv_w, a_log, dt_bias, gdn_norm_w, conv_sc_w, norm_mix_post,
                   norm_ffn_pre, norm_ffn_post):
    lane_row = lambda v: jnp.zeros((1, LANES), F32).at[0, A_LANE0:A_LANE0 + HEADS].set(v.astype(F32))
    return {
        "norm_mix_pre": norm_mix_pre[l].reshape(1, D_MODEL),
        "conv_qkv_w": conv_qkv_w[l],
        "a_log": lane_row(a_log[l]),
        "dt_bias": lane_row(dt_bias[l]),
        "gdn_norm_w": gdn_norm_w[l].reshape(1, HEAD_DIM),
        "conv_sc_w": conv_sc_w[l],
        "norm_mix_post": norm_mix_post[l].reshape(1, D_MODEL),
        "norm_ffn_pre": norm_ffn_pre[l].reshape(1, D_MODEL),
        "norm_ffn_post": norm_ffn_post[l].reshape(1, D_MODEL),
    }


def _mixer_tiling(batch, seq):
    chunk = min(CHUNK, seq)
    if seq <= CHUNK:
        return batch, seq, chunk
    return batch, MIXER_TILE, chunk


def _run(x_main, caches_main, x_extra, caches_extra, params, stacks):
    groups = [(x_main, caches_main), (x_extra, caches_extra)]
    tilings = [_mixer_tiling(x.shape[0], x.shape[1]) for x, _ in groups]
    xs = [x for x, _ in groups]
    outs = [([], [], []) for _ in groups]
    for l, p in enumerate(params):
        for gi, (_, caches) in enumerate(groups):
            nb, tile, chunk = tilings[gi]
            xs[gi], c, st, sc = _mixer(xs[gi], *caches, p, stacks, l, nb=nb, tile=tile, chunk=chunk)
            for acc, new in zip(outs[gi], (c, st, sc)):
                acc.append(new)
        flat = [x.reshape(-1, D_MODEL) for x in xs]
        ys = _ffn(flat[0], flat[1], p, stacks, l, tile=min(FFN_TILE, flat[0].shape[0]))
        xs = [y.reshape(x.shape) for y, x in zip(ys, xs)]
    return [(x,) + tuple(jnp.stack(a) for a in acc) for x, acc in zip(xs, outs)]


def kernel(x_prompt, x_sample, cache_gdn_conv, state_gdn, cache_sc_conv, norm_mix_pre, w_in, conv_qkv_w, a_log, dt_bias, gdn_norm_w, conv_sc_w, w_o, norm_mix_post, norm_ffn_pre, w_gate, w_up, w_down, norm_ffn_post):
    depth = w_in.shape[0]
    stacks = _weight_stacks(w_in, w_o, w_gate, w_up, w_down)
    params = [_layer_vectors(l, norm_mix_pre, conv_qkv_w, a_log, dt_bias, gdn_norm_w, conv_sc_w,
                             norm_mix_post, norm_ffn_pre, norm_ffn_post) for l in range(depth)]
    bp = x_prompt.shape[0]
    zc = jnp.zeros((depth, bp, GDN_CONV - 1, QKV_W), F32)
    zs = jnp.zeros((depth, bp, HEADS, HEAD_DIM, HEAD_DIM), F32)
    zsc = jnp.zeros((depth, bp, SC_CONV - 1, SC_W), F32)
    (y_prompt, conv_p, state_p, sc_p), (y_sample, conv_s, state_s, sc_s) = _run(
        x_prompt, (zc, zs, zsc), x_sample, (cache_gdn_conv, state_gdn, cache_sc_conv), params, stacks)
    return (y_prompt, y_sample, conv_p, state_p, sc_p, conv_s, state_s, sc_s)
```

```python
import functools

import jax
import jax.numpy as jnp
from jax import lax
from jax.experimental import pallas as pl
from jax.experimental.pallas import tpu as pltpu

F32 = jnp.float32
BF16 = jnp.bfloat16

D_MODEL = 1024
HEADS = 4
HEAD_DIM = 128
GDN_W = HEADS * HEAD_DIM
SC_W = D_MODEL - GDN_W
QKV_W = 3 * GDN_W
GDN_CONV = 4
SC_CONV = 3
D_FF = 2816
CHUNK = 64
EPS = 1e-6

QKVZ_W = QKV_W + GDN_W
GATE_COLS = 2 * HEADS
W_IN_CAST_COLS = 256
LANES = 128
SUBLANES = 8
A_LANE0 = HEADS

VMEM_LIMIT_BYTES = 56 * 1024 * 1024
MXU_TILE = 256
FFN_SLABS = (6 * MXU_TILE, 5 * MXU_TILE)
assert sum(FFN_SLABS) == D_FF
FFN_TILE = 1024
FFN_ROW_PARTS = 2
FFN_MIN_PART_ROWS = 2 * MXU_TILE
MIXER_TILE = 512
PREP_CHUNKS = 4
MAX_UNROLLED_TRIPS = 8


def _rms(x, g):
    return x * lax.rsqrt(jnp.mean(x * x, axis=-1, keepdims=True) + EPS) * g


def _silu(x):
    return x * jax.nn.sigmoid(x)


def _mm(a, b):
    return jnp.dot(a, b, preferred_element_type=F32)


def _aligned(index, multiple):
    return index if isinstance(index, int) else pl.multiple_of(index, multiple)


def _loop(trips, body):
    if trips <= MAX_UNROLLED_TRIPS:
        for i in range(trips):
            body(i, 0)
    else:
        lax.fori_loop(0, trips, body, 0)


def _split2(x):
    hi = x.astype(BF16)
    lo = (x - hi.astype(F32)).astype(BF16)
    return hi, lo


def _split3(x):
    h1 = x.astype(BF16)
    r1 = x - h1.astype(F32)
    h2 = r1.astype(BF16)
    h3 = (r1 - h2.astype(F32)).astype(BF16)
    return h1, h2, h3


def _unit_lower_inverses(a_strict, row, col, size):
    eye = (row == col).astype(F32)
    ts = [eye - jnp.where((row >> 1) == (col >> 1), a, 0.0) for a in a_strict]
    a16 = [a.astype(BF16) for a in a_strict]
    shift = 1
    while (1 << shift) < size:
        off = ((row >> (shift + 1)) == (col >> (shift + 1))) & ((row >> shift) != (col >> shift))
        t16 = [t.astype(BF16) for t in ts]
        inner = [_mm(jnp.where(off, a, jnp.zeros_like(a)), t) for a, t in zip(a16, t16)]
        outer = [_mm(t, i.astype(BF16)) for t, i in zip(t16, inner)]
        ts = [t - o for t, o in zip(ts, outer)]
        shift += 1
    return ts


def _solve_refined(a_strict, t_approx, rhs):
    t16 = [t.astype(BF16) for t in t_approx]
    x0 = [_mm(t, r.astype(BF16)).astype(BF16) for t, r in zip(t16, rhs)]
    a_parts = [_split2(a) for a in a_strict]
    ax_hi = [_mm(a[0], x) for a, x in zip(a_parts, x0)]
    ax_lo = [_mm(a[1], x) for a, x in zip(a_parts, x0)]
    res = [r - x.astype(F32) - (h + l) for r, x, h, l in zip(rhs, x0, ax_hi, ax_lo)]
    corr = [_mm(t, r.astype(BF16)) for t, r in zip(t16, res)]
    return [x.astype(F32) + c for x, c in zip(x0, corr)]


def _causal_dwconv(head_ref, b, x, w, tail_ref):
    width = w.shape[0]
    n = x.shape[0]
    xp = jnp.concatenate([head_ref[b], x], axis=0)
    acc = pltpu.roll(xp, width - 1, axis=0)[SUBLANES:] * w[0:1]
    for i in range(1, width - 1):
        acc = acc + pltpu.roll(xp, width - 1 - i, axis=0)[SUBLANES:] * w[i:i + 1]
    acc = acc + x * w[width - 1:width]
    head_ref[b] = xp[n:]
    tail_ref[b] = xp[n + SUBLANES - (width - 1):]
    return acc


def _mixer_kernel(x_ref, cprev_ref, s0_ref, scprev_ref, gpre_ref, wqkvz_ref, wsc_ref, wba_ref, cw_ref, alog_ref,
                  dtb_ref, gnw_ref, scw_ref, wo_ref, gpost_ref,
                  y_ref, cnew_ref, snew_ref, scnew_ref,
                  s_scr, chead, schead, q_scr, k_scr, v_scr, g_scr, beta_scr, o_scr,
                  gc_scr, u_scr, wq_scr, qkk_scr, zg_scr, sc_scr,
                  *, nb, tile, chunk):
    rows = nb * tile

    @pl.when(pl.program_id(1) == 0)
    def _():
        s_scr[...] = s0_ref[...]
        chead[...] = jnp.zeros(chead.shape, F32)
        schead[...] = jnp.zeros(schead.shape, F32)
        for b in range(nb):
            chead[b, SUBLANES - (GDN_CONV - 1):, :] = cprev_ref[b]
            schead[b, SUBLANES - (SC_CONV - 1):, :] = scprev_ref[b]

    x = x_ref[...].reshape(rows, D_MODEL)
    h = _rms(x, gpre_ref[...]).astype(BF16)

    qkv_in = _mm(h, wqkvz_ref[:, :QKV_W])
    pba = _mm(h, wba_ref[...])
    z_in = _mm(h, wqkvz_ref[:, QKV_W:])
    sc_c = _mm(h, wsc_ref[:, SC_W:2 * SC_W])
    sc_h = _mm(h, wsc_ref[:, 2 * SC_W:])
    sc_b = _mm(h, wsc_ref[:, :SC_W])

    cw = cw_ref[...]
    for b in range(nb):
        r0 = b * tile
        acc = _causal_dwconv(chead, b, qkv_in[r0:r0 + tile], cw, cnew_ref)
        act = _silu(acc)
        for hh in range(HEADS):
            lo = hh * HEAD_DIM
            qh = act[:, lo:lo + HEAD_DIM]
            kh = act[:, GDN_W + lo:GDN_W + lo + HEAD_DIM]
            q_scr[r0:r0 + tile, lo:lo + HEAD_DIM] = (
                qh * lax.rsqrt(jnp.sum(qh * qh, axis=-1, keepdims=True) + EPS) * (HEAD_DIM ** -0.5))
            k_scr[r0:r0 + tile, lo:lo + HEAD_DIM] = (
                kh * lax.rsqrt(jnp.sum(kh * kh, axis=-1, keepdims=True) + EPS))
        v_scr[r0:r0 + tile, :] = act[:, 2 * GDN_W:]

    beta_scr[...] = jax.nn.sigmoid(pba)
    ga = pba + dtb_ref[...]
    softplus = jnp.maximum(ga, 0.0) + jnp.log1p(jnp.exp(-jnp.abs(ga)))
    g_scr[...] = -jnp.exp(alog_ref[...]) * softplus

    row = lax.broadcasted_iota(jnp.int32, (chunk, chunk), 0)
    col = lax.broadcasted_iota(jnp.int32, (chunk, chunk), 1)
    causal = row >= col
    strict = row > col
    tri = causal.astype(BF16)

    zg_scr[...] = _silu(z_in)
    cm = sc_c * sc_h
    scw = scw_ref[...]
    for b in range(nb):
        r0 = b * tile
        sc_scr[r0:r0 + tile, :] = sc_b[r0:r0 + tile] * _causal_dwconv(schead, b, cm[r0:r0 + tile], scw, scnew_ref)

    n_chunks = tile // chunk
    cpi = min(PREP_CHUNKS, n_chunks)
    units = [(b, k) for b in range(nb) for k in range(cpi)]
    prep_probs = [(ui, hh) for ui in range(len(units)) for hh in range(HEADS)]
    chain_probs = [(b, hh) for b in range(nb) for hh in range(HEADS)]

    def prep_step(it, carry):
        cidx = [it * cpi + k for _, k in units]
        r0s = [_aligned(b * tile + ci * chunk, chunk) for (b, _), ci in zip(units, cidx)]
        g3 = [_split3(g_scr[pl.ds(r0, chunk), :]) for r0 in r0s]
        gcs = [_mm(tri, t[0]) + (_mm(tri, t[1]) + _mm(tri, t[2])) for t in g3]
        gcts = [gc.T for gc in gcs]
        betas = [beta_scr[pl.ds(r0, chunk), :] for r0 in r0s]
        for r0, gc in zip(r0s, gcs):
            gc_scr[pl.ds(r0, chunk), :] = gc
        ops = []
        for ui, hh in prep_probs:
            lo = hh * HEAD_DIM
            gcol = gcs[ui][:, A_LANE0 + hh:A_LANE0 + hh + 1]
            grow = gcts[ui][A_LANE0 + hh:A_LANE0 + hh + 1, :]
            bcol = betas[ui][:, hh:hh + 1]
            qh = q_scr[pl.ds(r0s[ui], chunk), lo:lo + HEAD_DIM]
            kh = k_scr[pl.ds(r0s[ui], chunk), lo:lo + HEAD_DIM]
            vh = v_scr[pl.ds(r0s[ui], chunk), lo:lo + HEAD_DIM]
            kb = kh * bcol
            eg = jnp.exp(gcol)
            k_t = kh.T
            ops.append(dict(
                decay=jnp.exp(jnp.where(causal, gcol - grow, -jnp.inf)),
                kbq16=jnp.concatenate([kb, qh], axis=0).astype(BF16), kt16=k_t.astype(BF16),
                rhs=jnp.concatenate([vh * bcol, kb * eg], axis=1),
                qg=qh * eg,
                k_tail_t=k_t * jnp.exp(gcol[chunk - 1:chunk, :] - grow)))
        kq = [_mm(o["kbq16"], o["kt16"]) for o in ops]
        a_mats = [jnp.where(strict, p[:chunk] * o["decay"], 0.0) for p, o in zip(kq, ops)]
        qks = [p[chunk:] * o["decay"] for p, o in zip(kq, ops)]
        sols = _solve_refined(a_mats, _unit_lower_inverses(a_mats, row, col, chunk), [o["rhs"] for o in ops])
        for (ui, hh), o, qk, sol in zip(prep_probs, ops, qks, sols):
            lo = hh * HEAD_DIM
            r0 = r0s[ui]
            u_scr[pl.ds(r0, chunk), lo:lo + HEAD_DIM] = sol[:, :HEAD_DIM]
            wq_scr[pl.ds(2 * r0, 2 * chunk), lo:lo + HEAD_DIM] = jnp.concatenate(
                [sol[:, HEAD_DIM:], o["qg"]], axis=0).astype(BF16)
            qkk_scr[(units[ui][0] * n_chunks + cidx[ui]) * HEADS + hh] = jnp.concatenate(
                [qk, o["k_tail_t"]], axis=0).astype(BF16)
        return carry

    def chain_step(c, carry):
        r0s = [_aligned(b * tile + c * chunk, chunk) for b in range(nb)]
        s_old = [s_scr[b, hh] for b, hh in chain_probs]
        ws = [_mm(wq_scr[pl.ds(2 * r0s[b], 2 * chunk), hh * HEAD_DIM:(hh + 1) * HEAD_DIM], s.astype(BF16))
              for (b, hh), s in zip(chain_probs, s_old)]
        v_new = [(u_scr[pl.ds(r0s[b], chunk), hh * HEAD_DIM:(hh + 1) * HEAD_DIM] - w[:chunk]).astype(BF16)
                 for (b, hh), w in zip(chain_probs, ws)]
        r2 = [_mm(qkk_scr[(b * n_chunks + c) * HEADS + hh], v) for (b, hh), v in zip(chain_probs, v_new)]
        for (b, hh), s, w, r in zip(chain_probs, s_old, ws, r2):
            lo = hh * HEAD_DIM
            o_scr[pl.ds(r0s[b], chunk), lo:lo + HEAD_DIM] = w[chunk:] + r[:chunk]
            glast = gc_scr[pl.ds(r0s[b] + (chunk - 1), 1), :][:, A_LANE0 + hh:A_LANE0 + hh + 1]
            s_scr[b, hh] = s * jnp.exp(glast) + r[chunk:]
        return carry

    if n_chunks <= MAX_UNROLLED_TRIPS:
        for it in range(n_chunks // cpi):
            prep_step(it, 0)
            for c in range(it * cpi, (it + 1) * cpi):
                chain_step(c, 0)
    else:
        _loop(n_chunks // cpi, prep_step)
        _loop(n_chunks, chain_step)
    snew_ref[...] = s_scr[...]

    gnw = gnw_ref[...]
    og = []
    for hh in range(HEADS):
        lo = hh * HEAD_DIM
        og.append(_rms(o_scr[:, lo:lo + HEAD_DIM], gnw) * zg_scr[:, lo:lo + HEAD_DIM])
    mix = _mm(jnp.concatenate(og + [sc_scr[...]], axis=1).astype(BF16), wo_ref[...])
    y_ref[...] = (x_ref[...].reshape(rows, D_MODEL) + _rms(mix, gpost_ref[...])).reshape(nb, tile, D_MODEL)


def _ffn_kernel(x_ref, gpre_ref, wg_ref, wu_ref, wd_ref, gpost_ref, y_ref):
    rows = x_ref.shape[0]
    parts = FFN_ROW_PARTS if rows >= FFN_ROW_PARTS * FFN_MIN_PART_ROWS else 1
    span = rows // parts
    hs = [_rms(x_ref[i * span:(i + 1) * span, :], gpre_ref[...]).astype(BF16) for i in range(parts)]
    for i, h in enumerate(hs):
        f = None
        lo = 0
        for width in FFN_SLABS:
            act = (_silu(_mm(h, wg_ref[:, lo:lo + width])) * _mm(h, wu_ref[:, lo:lo + width])).astype(BF16)
            part = _mm(act, wd_ref[lo:lo + width, :])
            f = part if f is None else f + part
            lo += width
        y_ref[i * span:(i + 1) * span, :] = x_ref[i * span:(i + 1) * span, :] + _rms(f, gpost_ref[...])


def _const_spec(shape):
    return pl.BlockSpec(shape, lambda *_: (0,) * len(shape))


def _layer_spec(layer, shape):
    return pl.BlockSpec((None,) + shape, lambda *_: (layer, 0, 0))


def _mixer(x, conv_prev, s0, sc_prev, p, stacks, layer, *, nb, tile, chunk):
    batch, seq, _ = x.shape
    rows = nb * tile
    grid = (batch // nb, seq // tile)
    per_b = lambda i, j: (i, 0, 0)
    in_specs = [
        pl.BlockSpec((nb, tile, D_MODEL), lambda i, j: (i, j, 0)),
        pl.BlockSpec((None, nb, GDN_CONV - 1, QKV_W), lambda i, j: (layer, i, 0, 0)),
        pl.BlockSpec((None, nb, HEADS, HEAD_DIM, HEAD_DIM), lambda i, j: (layer, i, 0, 0, 0)),
        pl.BlockSpec((None, nb, SC_CONV - 1, SC_W), lambda i, j: (layer, i, 0, 0)),
        _const_spec((1, D_MODEL)),
        _layer_spec(layer, (D_MODEL, QKVZ_W)),
        _layer_spec(layer, (D_MODEL, 3 * SC_W)),
        _layer_spec(layer, (D_MODEL, LANES)),
        _const_spec((GDN_CONV, QKV_W)),
        _const_spec((1, LANES)),
        _const_spec((1, LANES)),
        _const_spec((1, HEAD_DIM)),
        _const_spec((SC_CONV, SC_W)),
        _layer_spec(layer, (D_MODEL, D_MODEL)),
        _const_spec((1, D_MODEL)),
    ]
    out_specs = [
        pl.BlockSpec((nb, tile, D_MODEL), lambda i, j: (i, j, 0)),
        pl.BlockSpec((nb, GDN_CONV - 1, QKV_W), per_b),
        pl.BlockSpec((nb, HEADS, HEAD_DIM, HEAD_DIM), lambda i, j: (i, 0, 0, 0)),
        pl.BlockSpec((nb, SC_CONV - 1, SC_W), per_b),
    ]
    out_shape = [
        jax.ShapeDtypeStruct(x.shape, F32),
        jax.ShapeDtypeStruct(conv_prev.shape[1:], F32),
        jax.ShapeDtypeStruct(s0.shape[1:], F32),
        jax.ShapeDtypeStruct(sc_prev.shape[1:], F32),
    ]
    scratch = [
        pltpu.VMEM((nb, HEADS, HEAD_DIM, HEAD_DIM), F32),
        pltpu.VMEM((nb, SUBLANES, QKV_W), F32),
        pltpu.VMEM((nb, SUBLANES, SC_W), F32),
        pltpu.VMEM((rows, GDN_W), F32),
        pltpu.VMEM((rows, GDN_W), F32),
        pltpu.VMEM((rows, GDN_W), F32),
        pltpu.VMEM((rows, LANES), F32),
        pltpu.VMEM((rows, LANES), F32),
        pltpu.VMEM((rows, GDN_W), F32),
        pltpu.VMEM((rows, LANES), F32),
        pltpu.VMEM((rows, GDN_W), F32),
        pltpu.VMEM((2 * rows, GDN_W), BF16),
        pltpu.VMEM((rows // chunk * HEADS, chunk + HEAD_DIM, chunk), BF16),
        pltpu.VMEM((rows, GDN_W), F32),
        pltpu.VMEM((rows, SC_W), F32),
    ]
    return pl.pallas_call(
        functools.partial(_mixer_kernel, nb=nb, tile=tile, chunk=chunk),
        grid=grid,
        in_specs=in_specs,
        out_specs=out_specs,
        out_shape=out_shape,
        scratch_shapes=scratch,
        compiler_params=pltpu.CompilerParams(
            dimension_semantics=("arbitrary", "arbitrary"),
            vmem_limit_bytes=VMEM_LIMIT_BYTES),
        name=f"mixer_t{tile}_c{chunk}",
    )(x, conv_prev, s0, sc_prev, p["norm_mix_pre"], stacks["w_in"], stacks["w_sc"], stacks["w_ba"],
      p["conv_qkv_w"], p["a_log"],
      p["dt_bias"], p["gdn_norm_w"], p["conv_sc_w"], stacks["w_o"], p["norm_mix_post"])


def _ffn(x2d, p, stacks, layer, *, tile):
    rows = x2d.shape[0]
    row_spec = pl.BlockSpec((tile, D_MODEL), lambda i: (i, 0))
    return pl.pallas_call(
        _ffn_kernel,
        grid=(rows // tile,),
        in_specs=[
            row_spec,
            _const_spec((1, D_MODEL)),
            _layer_spec(layer, (D_MODEL, D_FF)),
            _layer_spec(layer, (D_MODEL, D_FF)),
            _layer_spec(layer, (D_FF, D_MODEL)),
            _const_spec((1, D_MODEL)),
        ],
        out_specs=row_spec,
        out_shape=jax.ShapeDtypeStruct(x2d.shape, F32),
        compiler_params=pltpu.CompilerParams(
            dimension_semantics=("parallel",),
            vmem_limit_bytes=VMEM_LIMIT_BYTES),
        name=f"ffn_t{tile}",
    )(x2d, p["norm_ffn_pre"], stacks["w_gate"], stacks["w_up"], stacks["w_down"], p["norm_ffn_post"])


def _w_in_groups_kernel(wq_ref, wg_ref, ws_ref, qkvz_ref, sc_ref, ba_ref):
    qkvz_ref[...] = wq_ref[...].T.astype(BF16)
    sc_ref[...] = ws_ref[0].T.astype(BF16)
    gate = jnp.concatenate([wg_ref[...], jnp.zeros((LANES - GATE_COLS, wg_ref.shape[1]), F32)], axis=0)
    ba_ref[...] = gate.T.astype(BF16)


def _w_in_groups(w_in):
    depth, d_in, _ = w_in.shape
    blk = W_IN_CAST_COLS
    n_q, n_s = QKVZ_W // blk, 3 * SC_W // blk
    sc_block = lambda r: jnp.minimum(r, n_s - 1)
    sc_row0 = QKVZ_W + GATE_COLS
    w_t = jnp.swapaxes(w_in, 1, 2)
    return pl.pallas_call(
        _w_in_groups_kernel,
        grid=(depth, n_q),
        in_specs=[
            pl.BlockSpec((None, blk, d_in), lambda l, r: (l, r, 0)),
            pl.BlockSpec((None, GATE_COLS, d_in), lambda l, r: (l, QKVZ_W // GATE_COLS, 0)),
            pl.BlockSpec((pl.Element(1), pl.Element(blk), pl.Element(d_in)),
                         lambda l, r: (l, pl.multiple_of(sc_row0 + blk * sc_block(r), SUBLANES), 0)),
        ],
        out_specs=[
            pl.BlockSpec((None, d_in, blk), lambda l, r: (l, 0, r)),
            pl.BlockSpec((None, d_in, blk), lambda l, r: (l, 0, sc_block(r))),
            pl.BlockSpec((None, d_in, LANES), lambda l, r: (l, 0, 0)),
        ],
        out_shape=[jax.ShapeDtypeStruct((depth, d_in, QKVZ_W), BF16),
                   jax.ShapeDtypeStruct((depth, d_in, 3 * SC_W), BF16),
                   jax.ShapeDtypeStruct((depth, d_in, LANES), BF16)],
        name="w_in_groups",
    )(w_t, w_t, w_t)


def _weight_stacks(w_in, w_o, w_gate, w_up, w_down):
    w_qkvz, w_sc, w_ba = _w_in_groups(w_in)
    return {"w_in": w_qkvz, "w_sc": w_sc, "w_ba": w_ba, "w_o": w_o.astype(BF16),
            "w_gate": w_gate.astype(BF16), "w_up": w_up.astype(BF16), "w_down": w_down.astype(BF16)}


def _layer_vectors(l, norm_mix_pre, conv_qkv_w, a_log, dt_bias, gdn_norm_w, conv_sc_w, norm_mix_post,
                   norm_ffn_pre, norm_ffn_post):
    lane_row = lambda v: jnp.zeros((1, LANES), F32).at[0, A_LANE0:A_LANE0 + HEADS].set(v.astype(F32))
    return {
        "norm_mix_pre": norm_mix_pre[l].reshape(1, D_MODEL),
        "conv_qkv_w": conv_qkv_w[l],
        "a_log": lane_row(a_log[l]),
        "dt_bias": lane_row(dt_bias[l]),
        "gdn_norm_w": gdn_norm_w[l].reshape(1, HEAD_DIM),
        "conv_sc_w": conv_sc_w[l],
        "norm_mix_post": norm_mix_post[l].reshape(1, D_MODEL),
        "norm_ffn_pre": norm_ffn_pre[l].reshape(1, D_MODEL),
        "norm_ffn_post": norm_ffn_post[l].reshape(1, D_MODEL),
    }


def _mixer_tiling(batch, seq):
    chunk = min(CHUNK, seq)
    if seq <= CHUNK:
        return batch, seq, chunk
    return batch, MIXER_TILE, chunk


def _run(x, conv0, s0, sc0, params, stacks):
    batch, seq, _ = x.shape
    nb, tile, chunk = _mixer_tiling(batch, seq)
    ffn_tile = min(FFN_TILE, batch * seq)
    convs, states, scs = [], [], []
    for l, p in enumerate(params):
        x, c, s, sc = _mixer(x, conv0, s0, sc0, p, stacks, l, nb=nb, tile=tile, chunk=chunk)
        x = _ffn(x.reshape(batch * seq, D_MODEL), p, stacks, l, tile=ffn_tile).reshape(batch, seq, D_MODEL)
        convs.append(c)
        states.append(s)
        scs.append(sc)
    return x, jnp.stack(convs), jnp.stack(states), jnp.stack(scs)


def kernel(x_prompt, x_sample, cache_gdn_conv, state_gdn, cache_sc_conv, norm_mix_pre, w_in, conv_qkv_w, a_log, dt_bias, gdn_norm_w, conv_sc_w, w_o, norm_mix_post, norm_ffn_pre, w_gate, w_up, w_down, norm_ffn_post):
    depth = w_in.shape[0]
    stacks = _weight_stacks(w_in, w_o, w_gate, w_up, w_down)
    params = [_layer_vectors(l, norm_mix_pre, conv_qkv_w, a_log, dt_bias, gdn_norm_w, conv_sc_w,
                             norm_mix_post, norm_ffn_pre, norm_ffn_post) for l in range(depth)]
    bp = x_prompt.shape[0]
    zc = jnp.zeros((depth, bp, GDN_CONV - 1, QKV_W), F32)
    zs = jnp.zeros((depth, bp, HEADS, HEAD_DIM, HEAD_DIM), F32)
    zsc = jnp.zeros((depth, bp, SC_CONV - 1, SC_W), F32)
    y_prompt, conv_p, state_p, sc_p = _run(x_prompt, zc, zs, zsc, params, stacks)
    y_sample, conv_s, state_s, sc_s = _run(x_sample, cache_gdn_conv, state_gdn, cache_sc_conv, params, stacks)
    return (y_prompt, y_sample, conv_p, state_p, sc_p, conv_s, state_s, sc_s)
```
